```python
import jax
import jax.numpy as jnp
from jax import lax
import numpy as np

D_MODEL = 1024
BATCH = 8
SEQ = 4096
DEPTH = 1
DEC_BATCH = 128
DEC_SEQ = 1
PAST_LEN = 8192
PAGE_SIZE = 128

HEAD_DIM = 64
D_MIX = D_MODEL
FOX_HEADS = D_MIX // (2 * HEAD_DIM)
NSA_HEADS = D_MIX // (2 * HEAD_DIM)
NSA_KV_HEADS = 2
NSA_GROUP = NSA_HEADS // NSA_KV_HEADS
FOX_WIDTH = FOX_HEADS * HEAD_DIM
NSA_WIDTH = NSA_HEADS * HEAD_DIM
KV_WIDTH = NSA_KV_HEADS * HEAD_DIM
IN_SIZES = (FOX_WIDTH, FOX_WIDTH, FOX_WIDTH, FOX_HEADS, NSA_WIDTH, KV_WIDTH, KV_WIDTH, KV_WIDTH, KV_WIDTH, KV_WIDTH, KV_WIDTH, 3 * NSA_HEADS)
D_IN = sum(IN_SIZES)
ROPE_DIM = HEAD_DIM // 4
ROPE_THETA = 500000.0
Q_BLOCK = 128
CMP_BLOCK = 32
CMP_STRIDE = 16
CMP_HIDDEN = 128
SEL_BLOCK = 64
TOP_N = 8
N_LOCAL = 2
WINDOW = 512
N_GROUPS = 4
EXPERTS_PER_GROUP = 4
N_EXPERTS = N_GROUPS * EXPERTS_PER_GROUP
TOP_K_IN_GROUP = 2
D_EXPERT = 256
EPS = 1e-6
NEG = -1e30
FORCE = 1e4
SCALE = HEAD_DIM ** -0.5

kernel_name = "hymba_fox_nsa_hmoe_step"


def rms_norm(x, g):
    xf = x.astype(jnp.float32)
    y = xf * lax.rsqrt(jnp.mean(xf * xf, axis=-1, keepdims=True) + EPS)
    return (y * g.astype(jnp.float32)).astype(x.dtype)


def rope(x, pos):
    half = ROPE_DIM // 2
    inv = ROPE_THETA ** (-jnp.arange(half, dtype=jnp.float32) * 2.0 / ROPE_DIM)
    ang = pos.astype(jnp.float32)[:, None] * inv[None, :]
    cos = jnp.cos(ang)[:, None, :]
    sin = jnp.sin(ang)[:, None, :]
    xf = x[..., :ROPE_DIM].astype(jnp.float32)
    x1, x2 = xf[..., :half], xf[..., half:]
    rot = jnp.concatenate([x1 * cos - x2 * sin, x2 * cos + x1 * sin], axis=-1).astype(x.dtype)
    return jnp.concatenate([rot, x[..., ROPE_DIM:]], axis=-1)


def in_proj(x, pos, g_attn, w_in, b_forget, b_gate, g_q_fox, g_k_fox, g_q_nsa, g_k_sel, g_k_win):
    B, S, _ = x.shape
    z = rms_norm(x, g_attn) @ w_in
    offs = np.cumsum(IN_SIZES)[:-1].tolist()
    fq, fk, fv, ff, nq, kc, vc, ks, vs, kw, vw, ng = jnp.split(z, offs, axis=-1)

    def heads(t, n):
        return t.reshape(B, S, n, HEAD_DIM)

    fox_q = rms_norm(heads(fq, FOX_HEADS), g_q_fox)
    fox_kv = jnp.stack([rms_norm(heads(fk, FOX_HEADS), g_k_fox), heads(fv, FOX_HEADS)], axis=2)
    logf = jax.nn.log_sigmoid((ff + b_forget).astype(jnp.float32))
    nsa_q = rms_norm(heads(nq, NSA_HEADS), g_q_nsa)
    nsa_qr = rope(nsa_q, pos)
    cmp_kv = jnp.stack([heads(kc, NSA_KV_HEADS), heads(vc, NSA_KV_HEADS)], axis=2)
    sel_kv = jnp.stack([rope(rms_norm(heads(ks, NSA_KV_HEADS), g_k_sel), pos), heads(vs, NSA_KV_HEADS)], axis=2)
    win_kv = jnp.stack([rope(rms_norm(heads(kw, NSA_KV_HEADS), g_k_win), pos), heads(vw, NSA_KV_HEADS)], axis=2)
    gates = jax.nn.sigmoid(ng + b_gate).reshape(B, S, NSA_HEADS, 3)
    return fox_q, fox_kv, logf, nsa_q, nsa_qr, cmp_kv, sel_kv, win_kv, gates


def fox_partial(q, k, v, cq, ck, mask):
    s = jnp.einsum('bqhd,bshd->bhqs', q, k, preferred_element_type=jnp.float32) * SCALE
    bias = jnp.transpose(cq, (0, 2, 1))[..., :, None] - jnp.transpose(ck, (0, 2, 1))[..., None, :]
    s = jnp.where(mask, s + bias, NEG)
    m = jnp.max(s, axis=-1)
    p = jnp.exp(s - m[..., None])
    l = jnp.sum(p, axis=-1)
    o = jnp.einsum('bhqs,bshd->bhqd', p, v.astype(jnp.float32))
    return m, l, o


def merge_partials(a, b):
    m1, l1, o1 = a
    m2, l2, o2 = b
    m = jnp.maximum(m1, m2)
    a1 = jnp.exp(m1 - m)
    a2 = jnp.exp(m2 - m)
    return m, l1 * a1 + l2 * a2, o1 * a1[..., None] + o2 * a2[..., None]


def fox_finish(st, dtype):
    m, l, o = st
    out = (o / l[..., None]).transpose(0, 2, 1, 3)
    return out.reshape(out.shape[0], out.shape[1], FOX_WIDTH).astype(dtype)


def fox_prompt(fox_q, fox_kv, logf):
    B, S = fox_q.shape[:2]
    nb = S // Q_BLOCK
    c = jnp.cumsum(logf, axis=1)
    fk, fv = fox_kv[:, :, 0], fox_kv[:, :, 1]
    qb = fox_q.reshape(B, nb, Q_BLOCK, FOX_HEADS, HEAD_DIM).transpose(1, 0, 2, 3, 4)
    cb = c.reshape(B, nb, Q_BLOCK, FOX_HEADS).transpose(1, 0, 2, 3)
    qpos = jnp.arange(S, dtype=jnp.int32).reshape(nb, Q_BLOCK)
    kpos = jnp.arange(S, dtype=jnp.int32)

    def body(args):
        q_i, c_i, p_i = args
        mask = (kpos[None, :] <= p_i[:, None])[None, None]
        return fox_finish(fox_partial(q_i, fk, fv, c_i, c, mask), fox_q.dtype)

    out = lax.map(body, (qb, cb, qpos))
    return out.transpose(1, 0, 2, 3).reshape(B, S, FOX_WIDTH)


def fox_sample(fox_q, fox_kv, logf, cache_kv, cache_logf, page_table):
    DB, Qn = fox_q.shape[:2]
    n_pages = page_table.shape[1]
    past = n_pages * PAGE_SIZE
    logf_past = cache_logf[page_table].reshape(DB, past, FOX_HEADS).astype(jnp.float32)
    c_past = jnp.cumsum(logf_past, axis=1)
    c_new = c_past[:, -1:] + jnp.cumsum(logf, axis=1)
    causal = (jnp.arange(Qn)[None, :] <= jnp.arange(Qn)[:, None])[None, None]
    st = fox_partial(fox_q, fox_kv[:, :, 0], fox_kv[:, :, 1], c_new, c_new, causal)
    c_pages = c_past.reshape(DB, n_pages, PAGE_SIZE, FOX_HEADS).transpose(1, 0, 2, 3)

    def step(carry, xs):
        phys, c_pg = xs
        kv = cache_kv[phys]
        part = fox_partial(fox_q, kv[:, :, 0], kv[:, :, 1], c_new, c_pg, True)
        return merge_partials(carry, part), None

    st, _ = lax.scan(step, st, (page_table.T, c_pages))
    return fox_finish(st, fox_q.dtype)


def compress(rows, pos_emb, w1, w2):
    B, L = rows.shape[:2]
    nc = (L - CMP_BLOCK) // CMP_STRIDE + 1
    idx = np.arange(nc)[:, None] * CMP_STRIDE + np.arange(CMP_BLOCK)[None, :]
    blk = rows[:, idx] + pos_emb[:, None, :]
    flat = blk.transpose(0, 1, 3, 2, 4).reshape(B, nc, NSA_KV_HEADS, CMP_BLOCK * HEAD_DIM)
    return jax.nn.silu(flat @ w1) @ w2


def compress_kv(cmp_kv, pos_k, w_k1, w_k2, pos_v, w_v1, w_v2, g_k_cmp):
    kc = rms_norm(compress(cmp_kv[:, :, 0], pos_k, w_k1, w_k2), g_k_cmp)
    vc = compress(cmp_kv[:, :, 1], pos_v, w_v1, w_v2)
    return kc, vc


def cmp_to_sel(p, n_sel):
    lead = CMP_BLOCK // CMP_STRIDE - 1
    r = SEL_BLOCK // CMP_STRIDE
    span = r + lead
    nc = p.shape[-1]
    right = r * n_sel + span - lead - nc
    pp = jnp.pad(p, [(0, 0)] * (p.ndim - 1) + [(lead, right)])
    out = pp[..., 0:r * n_sel:r]
    for o in range(1, span):
        out = out + pp[..., o:o + r * n_sel:r]
    return out


def cmp_select(q, kc, vc, qpos, n_sel):
    B, Q = q.shape[:2]
    nc = kc.shape[1]
    qg = q.reshape(B, Q, NSA_KV_HEADS, NSA_GROUP, HEAD_DIM)
    s = jnp.einsum('bqkgd,bckd->bkgqc', qg, kc, preferred_element_type=jnp.float32) * SCALE
    cend = jnp.arange(nc) * CMP_STRIDE + (CMP_BLOCK - 1)
    valid = cend[None, :] <= qpos[:, None]
    s = jnp.where(valid, s, NEG)
    p = jnp.exp(s - jnp.max(s, axis=-1, keepdims=True)) * valid
    l = jnp.sum(p, axis=-1, keepdims=True)
    p = p / jnp.where(l > 0, l, 1.0)
    o = jnp.einsum('bkgqc,bckd->bqkgd', p, vc.astype(jnp.float32)).reshape(B, Q, NSA_HEADS, HEAD_DIM)
    imp = cmp_to_sel(jnp.sum(p, axis=2), n_sel)
    j = jnp.arange(n_sel)
    dist = (qpos // SEL_BLOCK)[:, None] - j[None, :]
    forced = (j[None, :] == 0) | ((dist >= 0) & (dist < N_LOCAL))
    score = jnp.where(dist >= 0, jnp.where(forced, FORCE, imp), NEG)
    _, idx = lax.top_k(score, min(TOP_N, n_sel))
    return o, idx


def sel_attend(q_rot, kb, vb, kpos, qpos):
    B, Q = q_rot.shape[:2]
    qg = q_rot.reshape(B, Q, NSA_KV_HEADS, NSA_GROUP, HEAD_DIM)
    s = jnp.einsum('bqkgd,bkqnrd->bkgqnr', qg, kb, preferred_element_type=jnp.float32) * SCALE
    mask = (kpos <= qpos[:, None, None])[:, :, None]
    s = jnp.where(mask, s, NEG)
    shp = s.shape
    p = jax.nn.softmax(s.reshape(shp[0], shp[1], shp[2], shp[3], -1), axis=-1).reshape(shp)
    o = jnp.einsum('bkgqnr,bkqnrd->bqkgd', p, vb.astype(jnp.float32))
    return o.reshape(B, Q, NSA_HEADS, HEAD_DIM)


def window_attend(q_rot, k, v, kpos, qpos):
    B, Q = q_rot.shape[:2]
    qg = q_rot.reshape(B, Q, NSA_KV_HEADS, NSA_GROUP, HEAD_DIM)
    s = jnp.einsum('bqkgd,bskd->bkgqs', qg, k, preferred_element_type=jnp.float32) * SCALE
    diff = qpos[:, None] - kpos[None, :]
    mask = (diff >= 0) & (diff < WINDOW) & (kpos[None, :] >= 0)
    p = jax.nn.softmax(jnp.where(mask, s, NEG), axis=-1)
    o = jnp.einsum('bkgqs,bskd->bqkgd', p, v.astype(jnp.float32))
    return o.reshape(B, Q, NSA_HEADS, HEAD_DIM)


def gate_combine(g, o_c, o_s, o_w):
    g = g.astype(jnp.float32)
    return g[..., 0:1] * o_c + g[..., 1:2] * o_s + g[..., 2:3] * o_w


def nsa_prompt(nsa_q, nsa_qr, cmp_kv, sel_kv, win_kv, gates, pos_k, w_k1, w_k2, pos_v, w_v1, w_v2, g_k_cmp):
    B, S = nsa_q.shape[:2]
    kc, vc = compress_kv(cmp_kv, pos_k, w_k1, w_k2, pos_v, w_v1, w_v2, g_k_cmp)
    n_sel = S // SEL_BLOCK
    blocks = sel_kv.reshape(B, n_sel, SEL_BLOCK, 2, NSA_KV_HEADS, HEAD_DIM)
    pad = -(-WINDOW // Q_BLOCK) * Q_BLOCK
    band_len = pad + Q_BLOCK
    win_p = jnp.pad(win_kv, ((0, 0), (pad, 0), (0, 0), (0, 0), (0, 0)))
    nb = S // Q_BLOCK
    b_ix = jnp.arange(B)[:, None, None, None]
    h_ix = jnp.arange(NSA_KV_HEADS)[None, :, None, None]

    def to_blocks(t):
        return t.reshape((B, nb, Q_BLOCK) + t.shape[2:]).swapaxes(0, 1)

    qpos = jnp.arange(S, dtype=jnp.int32).reshape(nb, Q_BLOCK)

    def body(args):
        q_i, qr_i, g_i, p_i = args
        o_c, idx = cmp_select(q_i, kc, vc, p_i, n_sel)
        sel = blocks[b_ix, idx, :, :, h_ix]
        kpos = idx[..., None] * SEL_BLOCK + jnp.arange(SEL_BLOCK)
        o_s = sel_attend(qr_i, sel[..., 0, :], sel[..., 1, :], kpos, p_i)
        start = p_i[0]
        band = lax.dynamic_slice_in_dim(win_p, start, band_len, axis=1)
        kpos_w = start - pad + jnp.arange(band_len)
        o_w = window_attend(qr_i, band[:, :, 0], band[:, :, 1], kpos_w, p_i)
        return gate_combine(g_i, o_c, o_s, o_w).reshape(B, Q_BLOCK, NSA_WIDTH).astype(nsa_q.dtype)

    out = lax.map(body, (to_blocks(nsa_q), to_blocks(nsa_qr), to_blocks(gates), qpos))
    return out.swapaxes(0, 1).reshape(B, S, NSA_WIDTH)


def nsa_sample(nsa_q, nsa_qr, cmp_kv, sel_kv, win_kv, gates, cache_cmp, cache_sel, state_win, page_table,
               pos_k, w_k1, w_k2, pos_v, w_v1, w_v2, g_k_cmp):
    DB, Qn = nsa_q.shape[:2]
    past = page_table.shape[1] * PAGE_SIZE
    qpos = past + jnp.arange(Qn, dtype=jnp.int32)
    cmp_all = jnp.concatenate([cache_cmp[page_table].reshape(DB, past, 2, NSA_KV_HEADS, HEAD_DIM), cmp_kv], axis=1)
    kc, vc = compress_kv(cmp_all, pos_k, w_k1, w_k2, pos_v, w_v1, w_v2, g_k_cmp)
    nb_past = past // SEL_BLOCK
    nnb = -(-Qn // SEL_BLOCK)
    n_sel = nb_past + nnb
    o_c, idx = cmp_select(nsa_q, kc, vc, qpos, n_sel)
    b_ix = jnp.arange(DB)[:, None, None, None]
    h_ix = jnp.arange(NSA_KV_HEADS)[None, :, None, None]
    bpp = PAGE_SIZE // SEL_BLOCK
    pool = cache_sel.reshape(-1, bpp, SEL_BLOCK, 2, NSA_KV_HEADS, HEAD_DIM)
    jp = jnp.minimum(idx, nb_past - 1)
    phys = page_table[b_ix, jp // bpp]
    past_g = pool[phys, jp % bpp, :, :, h_ix]
    new_blocks = jnp.pad(sel_kv, ((0, 0), (0, nnb * SEL_BLOCK - Qn), (0, 0), (0, 0), (0, 0)))
    new_blocks = new_blocks.reshape(DB, nnb, SEL_BLOCK, 2, NSA_KV_HEADS, HEAD_DIM)
    jn = jnp.clip(idx - nb_past, 0, nnb - 1)
    new_g = new_blocks[b_ix, jn, :, :, h_ix]
    sel = jnp.where((idx >= nb_past)[..., None, None, None], new_g, past_g)
    kpos = idx[..., None] * SEL_BLOCK + jnp.arange(SEL_BLOCK)
    o_s = sel_attend(nsa_qr, sel[..., 0, :], sel[..., 1, :], kpos, qpos)
    w_keep = state_win.shape[1]
    win_all = jnp.concatenate([state_win, win_kv], axis=1)
    kpos_w = past - w_keep + jnp.arange(w_keep + Qn)
    o_w = window_attend(nsa_qr, win_all[:, :, 0], win_all[:, :, 1], kpos_w, qpos)
    out = gate_combine(gates, o_c, o_s, o_w).reshape(DB, Qn, NSA_WIDTH).astype(nsa_q.dtype)
    return out, win_all[:, Qn:]


def hier_moe(h, w_rg, b_rg, w_re, b_re, w_gate, w_up, w_down):
    lg = (h @ w_rg + b_rg).astype(jnp.float32)
    pg = jax.nn.softmax(lg, axis=-1)
    gsel = jnp.argmax(lg, axis=-1)
    pg_sel = jnp.take_along_axis(pg, gsel[:, None], axis=1)
    le = (h @ w_re + b_re).astype(jnp.float32).reshape(-1, N_GROUPS, EXPERTS_PER_GROUP)
    le_g = jnp.take_along_axis(le, gsel[:, None, None], axis=1)[:, 0]
    tw, ti = lax.top_k(jax.nn.softmax(le_g, axis=-1), TOP_K_IN_GROUP)
    tw = tw / jnp.sum(tw, axis=-1, keepdims=True) * pg_sel
    eid = gsel[:, None] * EXPERTS_PER_GROUP + ti
    gate = jnp.einsum('nk,nke->ne', tw, jax.nn.one_hot(eid, N_EXPERTS, dtype=jnp.float32))
    hid = jax.nn.silu(jnp.einsum('nd,edf->nef', h, w_gate)) * jnp.einsum('nd,edf->nef', h, w_up)
    return jnp.einsum('nef,efd->nd', hid * gate[..., None].astype(hid.dtype), w_down)


def finish_layer(x, o_fox, o_nsa, g_out_fox, g_out_nsa, w_out, g_ffn, w_rg, b_rg, w_re, b_re, w_gate, w_up, w_down):
    mix = jnp.concatenate([rms_norm(o_fox, g_out_fox), rms_norm(o_nsa, g_out_nsa)], axis=-1)
    x = x + mix @ w_out
    B, S, D = x.shape
    y = hier_moe(rms_norm(x, g_ffn).reshape(B * S, D), w_rg, b_rg, w_re, b_re, w_gate, w_up, w_down)
    return x + y.reshape(B, S, D)


def setup_inputs(seed: int = 0) -> dict:
    key = jax.random.key(seed)
    ks = jax.random.split(key, 40)
    n_pages = PAST_LEN // PAGE_SIZE
    n_phys = (DEC_BATCH * n_pages * 5) // 4
    w_keep = min(WINDOW, PAST_LEN)
    f32 = jnp.float32

    def nrm(k, shape, scale):
        return jax.random.normal(k, shape, f32) * scale

    def gain(k, shape):
        return 1.0 + 0.05 * jax.random.normal(k, shape, f32)

    page_table = jax.random.permutation(ks[0], n_phys)[:DEC_BATCH * n_pages].reshape(DEC_BATCH, n_pages).astype(jnp.int32)
    return {
        "x_prompt": nrm(ks[1], (BATCH, SEQ, D_MODEL), 1.0),
        "x_sample": nrm(ks[2], (DEC_BATCH, DEC_SEQ, D_MODEL), 1.0),
        "cache_fox_kv": nrm(ks[3], (DEPTH, n_phys, PAGE_SIZE, 2, FOX_HEADS, HEAD_DIM), 1.0),
        "cache_fox_logf": jax.nn.log_sigmoid(3.0 + 0.5 * jax.random.normal(ks[4], (DEPTH, n_phys, PAGE_SIZE, FOX_HEADS), f32)),
        "cache_cmp_kv": nrm(ks[5], (DEPTH, n_phys, PAGE_SIZE, 2, NSA_KV_HEADS, HEAD_DIM), 1.0),
        "cache_sel_kv": nrm(ks[6], (DEPTH, n_phys, PAGE_SIZE, 2, NSA_KV_HEADS, HEAD_DIM), 1.0),
        "state_win_kv": nrm(ks[7], (DEPTH, DEC_BATCH, w_keep, 2, NSA_KV_HEADS, HEAD_DIM), 1.0),
        "page_table": page_table,
        "g_attn": gain(ks[8], (DEPTH, D_MODEL)),
        "w_in": nrm(ks[9], (DEPTH, D_MODEL, D_IN), D_MODEL ** -0.5),
        "b_forget": 3.0 + 0.3 * jax.random.normal(ks[10], (DEPTH, FOX_HEADS), f32),
        "b_gate": nrm(ks[11], (DEPTH, 3 * NSA_HEADS), 0.1),
        "g_q_fox": gain(ks[12], (DEPTH, HEAD_DIM)),
        "g_k_fox": gain(ks[13], (DEPTH, HEAD_DIM)),
        "g_q_nsa": gain(ks[14], (DEPTH, HEAD_DIM)),
        "g_k_cmp": gain(ks[15], (DEPTH, HEAD_DIM)),
        "g_k_sel": gain(ks[16], (DEPTH, HEAD_DIM)),
        "g_k_win": gain(ks[17], (DEPTH, HEAD_DIM)),
        "pos_cmp_k": nrm(ks[18], (DEPTH, CMP_BLOCK, HEAD_DIM), 0.5),
        "w_cmp_k1": nrm(ks[19], (DEPTH, CMP_BLOCK * HEAD_DIM, CMP_HIDDEN), (CMP_BLOCK * HEAD_DIM) ** -0.5),
        "w_cmp_k2": nrm(ks[20], (DEPTH, CMP_HIDDEN, HEAD_DIM), CMP_HIDDEN ** -0.5),
        "pos_cmp_v": nrm(ks[21], (DEPTH, CMP_BLOCK, HEAD_DIM), 0.5),
        "w_cmp_v1": nrm(ks[22], (DEPTH, CMP_BLOCK * HEAD_DIM, CMP_HIDDEN), (CMP_BLOCK * HEAD_DIM) ** -0.5),
        "w_cmp_v2": nrm(ks[23], (DEPTH, CMP_HIDDEN, HEAD_DIM), CMP_HIDDEN ** -0.5),
        "g_out_fox": gain(ks[24], (DEPTH, FOX_WIDTH)),
        "g_out_nsa": gain(ks[25], (DEPTH, NSA_WIDTH)),
        "w_out": nrm(ks[26], (DEPTH, D_MIX, D_MODEL), D_MIX ** -0.5),
        "g_ffn": gain(ks[27], (DEPTH, D_MODEL)),
        "w_router_group": nrm(ks[28], (DEPTH, D_MODEL, N_GROUPS), D_MODEL ** -0.5),
        "b_router_group": nrm(ks[29], (DEPTH, N_GROUPS), 0.01),
        "w_router_expert": nrm(ks[30], (DEPTH, D_MODEL, N_EXPERTS), D_MODEL ** -0.5),
        "b_router_expert": nrm(ks[31], (DEPTH, N_EXPERTS), 0.01),
        "w_exp_gate": nrm(ks[32], (DEPTH, N_EXPERTS, D_MODEL, D_EXPERT), D_MODEL ** -0.5),
        "w_exp_up": nrm(ks[33], (DEPTH, N_EXPERTS, D_MODEL, D_EXPERT), D_MODEL ** -0.5),
        "w_exp_down": nrm(ks[34], (DEPTH, N_EXPERTS, D_EXPERT, D_MODEL), D_EXPERT ** -0.5),
    }


def reference(x_prompt, x_sample, cache_fox_kv, cache_fox_logf, cache_cmp_kv, cache_sel_kv, state_win_kv, page_table,
              g_attn, w_in, b_forget, b_gate, g_q_fox, g_k_fox, g_q_nsa, g_k_cmp, g_k_sel, g_k_win,
              pos_cmp_k, w_cmp_k1, w_cmp_k2, pos_cmp_v, w_cmp_v1, w_cmp_v2, g_out_fox, g_out_nsa, w_out,
              g_ffn, w_router_group, b_router_group, w_router_expert, b_router_expert, w_exp_gate, w_exp_up, w_exp_down):
    xp, xs = x_prompt, x_sample
    seq = x_prompt.shape[1]
    past = page_table.shape[1] * PAGE_SIZE
    pos_p = jnp.arange(seq, dtype=jnp.int32)
    pos_s = past + jnp.arange(x_sample.shape[1], dtype=jnp.int32)
    fkv_p, fkv_s, flf_p, flf_s, ckv_p, ckv_s, skv_p, skv_s, wkv_p, wkv_s = ([] for _ in range(10))
    for l in range(DEPTH):
        proj_w = (g_attn[l], w_in[l], b_forget[l], b_gate[l], g_q_fox[l], g_k_fox[l], g_q_nsa[l], g_k_sel[l], g_k_win[l])
        cmp_w = (pos_cmp_k[l], w_cmp_k1[l], w_cmp_k2[l], pos_cmp_v[l], w_cmp_v1[l], w_cmp_v2[l], g_k_cmp[l])
        out_w = (g_out_fox[l], g_out_nsa[l], w_out[l], g_ffn[l], w_router_group[l], b_router_group[l],
                 w_router_expert[l], b_router_expert[l], w_exp_gate[l], w_exp_up[l], w_exp_down[l])
        fq, fkv, flf, nq, nqr, ckv, skv, wkv, gts = in_proj(xp, pos_p, *proj_w)
        o_fox = fox_prompt(fq, fkv, flf)
        o_nsa = nsa_prompt(nq, nqr, ckv, skv, wkv, gts, *cmp_w)
        fkv_p.append(fkv)
        flf_p.append(flf)
        ckv_p.append(ckv)
        skv_p.append(skv)
        wkv_p.append(wkv[:, seq - min(WINDOW, seq):])
        xp = finish_layer(xp, o_fox, o_nsa, *out_w)
        fq, fkv, flf, nq, nqr, ckv, skv, wkv, gts = in_proj(xs, pos_s, *proj_w)
        o_fox = fox_sample(fq, fkv, flf, cache_fox_kv[l], cache_fox_logf[l], page_table)
        o_nsa, new_win = nsa_sample(nq, nqr, ckv, skv, wkv, gts, cache_cmp_kv[l], cache_sel_kv[l], state_win_kv[l],
                                    page_table, *cmp_w)
        fkv_s.append(fkv)
        flf_s.append(flf)
        ckv_s.append(ckv)
        skv_s.append(skv)
        wkv_s.append(new_win)
        xs = finish_layer(xs, o_fox, o_nsa, *out_w)
    return (xp, xs, jnp.stack(fkv_p), jnp.stack(fkv_s), jnp.stack(flf_p), jnp.stack(flf_s),
            jnp.stack(ckv_p), jnp.stack(ckv_s), jnp.stack(skv_p), jnp.stack(skv_s), jnp.stack(wkv_p), jnp.stack(wkv_s))
```

```python
import functools

import jax
import jax.numpy as jnp
import numpy as np
from jax import lax
from jax.experimental import pallas as pl
from jax.experimental.pallas import tpu as pltpu

F32 = jnp.float32
BF16 = jnp.bfloat16

LANES = 128
HEAD_DIM = 64
FOX_HEADS = 8
NSA_HEADS = 8
NSA_KV_HEADS = 2
NSA_GROUP = NSA_HEADS // NSA_KV_HEADS
ROPE_DIM = HEAD_DIM // 4
ROPE_THETA = 500000.0
CMP_BLOCK = 32
CMP_STRIDE = 16
CMP_HIDDEN = 128
SEL_BLOCK = 64
TOP_N = 8
N_LOCAL = 2
WINDOW = 512
N_GROUPS = 4
EXPERTS_PER_GROUP = 4
N_EXPERTS = N_GROUPS * EXPERTS_PER_GROUP
EPS = 1e-6
NEG = -1e30
FORCE = 1e4
SCALE = HEAD_DIM ** -0.5
PAGE_SIZE = 128

FOX_W = FOX_HEADS * HEAD_DIM
NSA_W = NSA_HEADS * HEAD_DIM
KV_W = NSA_KV_HEADS * HEAD_DIM

C_FQ, C_FK, C_FV, C_NQ = 0, 512, 1024, 1536
C_KC, C_VC, C_KS, C_VS, C_KW, C_VW = 2048, 2176, 2304, 2432, 2560, 2688
C_SMALL = 2816
D_IN_PAD = C_SMALL + LANES
SM_LOGF, SM_GATE, SM_CUM = 0, 8, 32


VMEM_LIMIT = 56 * 1024 * 1024


def _params(grid_rank, vmem=VMEM_LIMIT):
    return pltpu.CompilerParams(dimension_semantics=("arbitrary",) * grid_rank, vmem_limit_bytes=vmem)


def _dot(a, b):
    return jnp.dot(a, b, preferred_element_type=F32)


def _dot_t(a, b):
    return lax.dot_general(a, b, (((1,), (1,)), ((), ())), preferred_element_type=F32)


def _split3(x):
    hi = x.astype(BF16)
    r = x - hi.astype(F32)
    mid = r.astype(BF16)
    lo = (r - mid.astype(F32)).astype(BF16)
    return hi, mid, lo


def _dot3(a_f32, b_bf16):
    hi, mid, lo = _split3(a_f32)
    return _dot(hi, b_bf16) + _dot(mid, b_bf16) + _dot(lo, b_bf16)


def _lane(shape):
    return lax.broadcasted_iota(jnp.int32, shape, len(shape) - 1)


def _head_rms(seg, bd, gain):
    ms = _dot((seg * seg).astype(BF16), bd) * (1.0 / HEAD_DIM)
    return seg * lax.rsqrt(ms + EPS) * gain


def _rope(x, cos, sin_m, sin_p):
    return x * cos + pltpu.roll(x, LANES - ROPE_DIM // 2, 1) * sin_m + pltpu.roll(x, ROPE_DIM // 2, 1) * sin_p


def _in_proj_common(x_ref, gat_ref, w_ref, bsm_ref, gq_fox_ref, gk_fox_ref, gq_nsa_ref, gk_sel_ref, gk_win_ref,
                    bd_ref, cos_ref, sinm_ref, sinp_ref):
    x = x_ref[...]
    xn = (x * lax.rsqrt(jnp.mean(x * x, axis=-1, keepdims=True) + EPS) * gat_ref[...]).astype(BF16)
    bd = bd_ref[...]
    bd128 = bd[:LANES, :LANES]
    cos, sin_m, sin_p = cos_ref[...], sinm_ref[...], sinp_ref[...]

    def proj(c0, width):
        return _dot(xn, w_ref[:, c0:c0 + width])

    def normed(c0, gain_ref, chunks):
        outs = []
        for j in range(chunks):
            seg = proj(c0 + 256 * j, 256)
            outs.append(_head_rms(seg, bd, gain_ref[...]))
        return outs

    out = {}
    out["fq"] = normed(C_FQ, gq_fox_ref, 2)
    out["fk"] = normed(C_FK, gk_fox_ref, 2)
    out["fv"] = [proj(C_FV, 256), proj(C_FV + 256, 256)]
    out["nq"] = normed(C_NQ, gq_nsa_ref, 2)
    out["nqr"] = [jnp.concatenate([_rope(c[:, :LANES], cos, sin_m, sin_p), _rope(c[:, LANES:], cos, sin_m, sin_p)], axis=1)
                  for c in out["nq"]]
    out["ckv"] = proj(C_KC, 256)
    ks = proj(C_KS, LANES)
    out["ks"] = _rope(ks * lax.rsqrt(_dot((ks * ks).astype(BF16), bd128) * (1.0 / HEAD_DIM) + EPS) * gk_sel_ref[...],
                      cos, sin_m, sin_p)
    out["vs"] = proj(C_VS, LANES)
    kw = proj(C_KW, LANES)
    out["kw"] = _rope(kw * lax.rsqrt(_dot((kw * kw).astype(BF16), bd128) * (1.0 / HEAD_DIM) + EPS) * gk_win_ref[...],
                      cos, sin_m, sin_p)
    out["vw"] = proj(C_VW, LANES)
    t = proj(C_SMALL, LANES) + bsm_ref[...]
    logf = jnp.minimum(t, 0.0) - jnp.log1p(jnp.exp(-jnp.abs(t)))
    gate = 1.0 / (1.0 + jnp.exp(-t))
    lane = _lane(t.shape)
    out["small"] = jnp.where(lane < SM_GATE, logf, jnp.where(lane < SM_CUM, gate, 0.0))
    return out


def _in_proj_prompt_kernel(x_ref, gat_ref, w_ref, bsm_ref, gq_fox_ref, gk_fox_ref, gq_nsa_ref, gk_sel_ref, gk_win_ref,
                           bd_ref, cos_ref, sinm_ref, sinp_ref, tri_ref,
                           fkv_ref, ckv_ref, skv_ref, wkv_ref, small_ref,
                           fq_ref, fk_ref, fv_ref, nq_ref, ksel_ref, vsel_ref, kwin_ref, vwin_ref,
                           carry_ref, *, tiles_per_seq):
    p = _in_proj_common(x_ref, gat_ref, w_ref, bsm_ref, gq_fox_ref, gk_fox_ref, gq_nsa_ref, gk_sel_ref, gk_win_ref,
                        bd_ref, cos_ref, sinm_ref, sinp_ref)
    tm = x_ref.shape[0]
    lane = _lane((tm, LANES))
    lo_half = lane < HEAD_DIM

    @pl.when(pl.program_id(0) % tiles_per_seq == 0)
    def _():
        carry_ref[...] = jnp.zeros_like(carry_ref)

    small = p["small"]
    hi, mid, lo = _split3(small)
    tri = tri_ref[...]
    cum = _dot(tri, hi) + _dot(tri, mid) + _dot(tri, lo) + carry_ref[...]
    carry_ref[...] = cum[tm - 1:tm, :]
    cum_sh = pltpu.roll(cum, SM_CUM - SM_LOGF, 1)
    small_ref[...] = jnp.where((lane >= SM_CUM) & (lane < SM_CUM + FOX_HEADS), cum_sh, small)

    fkv_ref[:, 0:256] = p["fk"][0]
    fkv_ref[:, 256:512] = p["fk"][1]
    fkv_ref[:, 512:768] = p["fv"][0]
    fkv_ref[:, 768:1024] = p["fv"][1]
    ckv_ref[...] = p["ckv"]
    skv_ref[:, 0:LANES] = p["ks"]
    skv_ref[:, LANES:2 * LANES] = p["vs"]
    wkv_ref[:, 0:LANES] = p["kw"]
    wkv_ref[:, LANES:2 * LANES] = p["vw"]

    def pair(chunks, j):
        c = chunks[j // 2]
        return c[:, (j % 2) * LANES:(j % 2 + 1) * LANES]

    one = jnp.ones((tm, LANES), F32)
    zero = jnp.zeros((tm, LANES), F32)
    ones_col = jnp.where(lane == HEAD_DIM, one, zero)
    for h in range(FOX_HEADS):
        qc, kc, vc = pair(p["fq"], h // 2), pair(p["fk"], h // 2), pair(p["fv"], h // 2)
        if h % 2:
            qc, kc, vc = pltpu.roll(qc, HEAD_DIM, 1), pltpu.roll(kc, HEAD_DIM, 1), pltpu.roll(vc, HEAD_DIM, 1)
        ch = jnp.broadcast_to(cum[:, SM_LOGF + h:SM_LOGF + h + 1], (tm, LANES))
        c_hi, c_mid, c_lo = _split3(ch)
        c_hi, c_mid, c_lo = c_hi.astype(F32), c_mid.astype(F32), c_lo.astype(F32)
        q_ext = jnp.where(lane == 64, c_hi, jnp.where(lane == 65, c_mid, jnp.where(lane == 66, c_lo,
                          jnp.where(lane < 70, one, zero))))
        k_ext = jnp.where(lane < 67, one, jnp.where(lane == 67, -c_hi, jnp.where(lane == 68, -c_mid,
                          jnp.where(lane == 69, -c_lo, zero))))
        fq_ref[0, h] = jnp.where(lo_half, qc * SCALE, q_ext).astype(BF16)
        fk_ref[0, h] = jnp.where(lo_half, kc, k_ext).astype(BF16)
        fv_ref[0, h] = jnp.where(lo_half, vc, ones_col).astype(BF16)
    for h in range(NSA_HEADS):
        a, r = pair(p["nq"], h // 2), pair(p["nqr"], h // 2)
        if h % 2:
            v = jnp.where(lo_half, pltpu.roll(a, HEAD_DIM, 1), r)
        else:
            v = jnp.where(lo_half, a, pltpu.roll(r, HEAD_DIM, 1))
        nq_ref[0, h] = (v * SCALE).astype(BF16)
    for name_k, name_v, k_ref, v_ref in (("ks", "vs", ksel_ref, vsel_ref), ("kw", "vw", kwin_ref, vwin_ref)):
        kk, vv = p[name_k], p[name_v]
        k_ref[0, 0] = jnp.where(lo_half, zero, pltpu.roll(kk, HEAD_DIM, 1)).astype(BF16)
        k_ref[0, 1] = jnp.where(lo_half, zero, kk).astype(BF16)
        v_ref[0, 0] = jnp.where(lo_half, vv, ones_col).astype(BF16)
        v_ref[0, 1] = jnp.where(lo_half, pltpu.roll(vv, HEAD_DIM, 1), ones_col).astype(BF16)


def _in_proj_sample_kernel(x_ref, gat_ref, w_ref, bsm_ref, gq_fox_ref, gk_fox_ref, gq_nsa_ref, gk_sel_ref, gk_win_ref,
                           bd_ref, cos_ref, sinm_ref, sinp_ref,
                           fkv_ref, skv_ref, wkv_ref, small_ref, fq_ref, nqh_ref, qmat_ref,
                           fkv_t_ref, ckv_t_ref, skv_t_ref, wkv_t_ref, small_t_ref):
    p = _in_proj_common(x_ref, gat_ref, w_ref, bsm_ref, gq_fox_ref, gk_fox_ref, gq_nsa_ref, gk_sel_ref, gk_win_ref,
                        bd_ref, cos_ref, sinm_ref, sinp_ref)
    small_ref[...] = p["small"]
    small_t_ref[...] = p["small"].T
    for j in range(2):
        fkv_ref[:, 256 * j:256 * (j + 1)] = p["fk"][j]
        fkv_ref[:, 512 + 256 * j:512 + 256 * (j + 1)] = p["fv"][j]
        fkv_t_ref[256 * j:256 * (j + 1), :] = p["fk"][j].T
        fkv_t_ref[512 + 256 * j:512 + 256 * (j + 1), :] = p["fv"][j].T
        fq_ref[:, 256 * j:256 * (j + 1)] = p["fq"][j] * SCALE
    ckv_t_ref[...] = p["ckv"].T
    for ref, ref_t, k_name, v_name in ((skv_ref, skv_t_ref, "ks", "vs"), (wkv_ref, wkv_t_ref, "kw", "vw")):
        ref[:, 0:LANES] = p[k_name]
        ref[:, LANES:2 * LANES] = p[v_name]
        ref_t[0:LANES, :] = p[k_name].T
        ref_t[LANES:2 * LANES, :] = p[v_name].T
    tm = x_ref.shape[0]
    lo_half = _lane((tm, LANES)) < HEAD_DIM
    zero = jnp.zeros((tm, LANES), F32)
    for h in range(NSA_HEADS):
        a = p["nq"][h // 4][:, (h // 2 % 2) * LANES:(h // 2 % 2 + 1) * LANES]
        r = p["nqr"][h // 4][:, (h // 2 % 2) * LANES:(h // 2 % 2 + 1) * LANES]
        if h % 2:
            v = jnp.where(lo_half, pltpu.roll(a, HEAD_DIM, 1), r)
        else:
            v = jnp.where(lo_half, a, pltpu.roll(r, HEAD_DIM, 1))
        nqh_ref[h] = (v * SCALE).astype(BF16)
        kvh = h // NSA_GROUP
        rr = r if (h % 2) == kvh else pltpu.roll(r, HEAD_DIM, 1)
        rr = jnp.where(lo_half, rr, zero) if kvh == 0 else jnp.where(lo_half, zero, rr)
        qmat_ref[h, :, 0:LANES] = (rr * SCALE).astype(BF16)
        qmat_ref[h, :, LANES:2 * LANES] = zero.astype(BF16)


def _const_spec(shape):
    nd = len(shape)
    return pl.BlockSpec(shape, lambda *_: (0,) * nd)


def _rope_tables(pos):
    half = ROPE_DIM // 2
    inv = ROPE_THETA ** (-jnp.arange(half, dtype=F32) * 2.0 / ROPE_DIM)
    ang = pos.astype(F32)[:, None] * inv[None, :]
    cos, sin = jnp.cos(ang), jnp.sin(ang)
    n = pos.shape[0]
    rest = HEAD_DIM - ROPE_DIM
    cos_h = jnp.concatenate([cos, cos, jnp.ones((n, rest), F32)], axis=1)
    sinm_h = jnp.concatenate([-sin, jnp.zeros((n, half + rest), F32)], axis=1)
    sinp_h = jnp.concatenate([jnp.zeros((n, half), F32), sin, jnp.zeros((n, rest), F32)], axis=1)
    return tuple(jnp.concatenate([t, t], axis=1) for t in (cos_h, sinm_h, sinp_h))


def _in_proj_operands(g_attn, w_in, b_forget, b_gate, g_q_fox, g_k_fox, g_q_nsa, g_k_sel, g_k_win):
    d_model = w_in.shape[0]
    sizes = (FOX_W, FOX_W, FOX_W, FOX_HEADS, NSA_W, KV_W, KV_W, KV_W, KV_W, KV_W, KV_W, 3 * NSA_HEADS)
    offs = np.concatenate([[0], np.cumsum(sizes)])
    fq, fk, fv, ff, nq, kc, vc, ks, vs, kw, vw, ng = [w_in[:, offs[i]:offs[i + 1]] for i in range(len(sizes))]
    pad = jnp.zeros((d_model, LANES - FOX_HEADS - 3 * NSA_HEADS), w_in.dtype)
    w = jnp.concatenate([fq, fk, fv, nq, kc, vc, ks, vs, kw, vw, ff, ng, pad], axis=1).astype(BF16)
    bsm = jnp.concatenate([b_forget, b_gate, jnp.zeros((LANES - FOX_HEADS - 3 * NSA_HEADS,), F32)]).reshape(1, LANES)
    hd = np.arange(256) // HEAD_DIM
    bd = jnp.asarray(hd[:, None] == hd[None, :], BF16)
    t256 = lambda g: jnp.tile(g, 256 // HEAD_DIM).reshape(1, 256)
    t128 = lambda g: jnp.tile(g, LANES // HEAD_DIM).reshape(1, LANES)
    return (g_attn.reshape(1, d_model), w, bsm, t256(g_q_fox), t256(g_k_fox), t256(g_q_nsa), t128(g_k_sel), t128(g_k_win), bd)


def _in_proj_prompt(x, ops, tm):
    b, s, d = x.shape
    n = b * s
    tps = s // tm
    cos, sinm, sinp = _rope_tables(jnp.arange(s, dtype=jnp.int32))
    tri = jnp.asarray(np.tril(np.ones((tm, tm), np.float32)), BF16)
    tab_spec = pl.BlockSpec((tm, LANES), lambda i: (i % tps, 0))
    in_specs = ([pl.BlockSpec((tm, d), lambda i: (i, 0))] + [_const_spec(o.shape) for o in ops]
                + [tab_spec, tab_spec, tab_spec, _const_spec((tm, tm))])
    row = lambda w: pl.BlockSpec((tm, w), lambda i: (i, 0))
    head = lambda nh: pl.BlockSpec((1, nh, tm, LANES), lambda i: (i // tps, 0, i % tps, 0))
    hshape = lambda nh: jax.ShapeDtypeStruct((b, nh, s, LANES), BF16)
    out_specs = [row(2 * FOX_W), row(2 * KV_W), row(2 * KV_W), row(2 * KV_W), row(LANES),
                 head(FOX_HEADS), head(FOX_HEADS), head(FOX_HEADS), head(NSA_HEADS),
                 head(NSA_KV_HEADS), head(NSA_KV_HEADS), head(NSA_KV_HEADS), head(NSA_KV_HEADS)]
    out_shape = [jax.ShapeDtypeStruct((n, 2 * FOX_W), F32)] + [jax.ShapeDtypeStruct((n, 2 * KV_W), F32)] * 3 + [
        jax.ShapeDtypeStruct((n, LANES), F32), hshape(FOX_HEADS), hshape(FOX_HEADS), hshape(FOX_HEADS), hshape(NSA_HEADS),
        hshape(NSA_KV_HEADS), hshape(NSA_KV_HEADS), hshape(NSA_KV_HEADS), hshape(NSA_KV_HEADS)]
    return pl.pallas_call(
        functools.partial(_in_proj_prompt_kernel, tiles_per_seq=tps),
        grid=(n // tm,), in_specs=in_specs, out_specs=out_specs, out_shape=out_shape,
        scratch_shapes=[pltpu.VMEM((1, LANES), F32)],
        compiler_params=_params(1),
        name="in_proj_prompt",
    )(x.reshape(n, d), *ops, cos, sinm, sinp, tri)


def _in_proj_sample(x, ops, past):
    b, q, d = x.shape
    n = b * q
    pos = past + jnp.arange(q, dtype=jnp.int32)
    cos, sinm, sinp = (jnp.tile(t, (b, 1)) for t in _rope_tables(pos))
    full = lambda shape: pl.BlockSpec(shape, lambda i: (0, 0))
    in_specs = [full((n, d))] + [_const_spec(o.shape) for o in ops] + [full((n, LANES))] * 3
    widths = (2 * FOX_W, 2 * KV_W, 2 * KV_W, LANES, FOX_W)
    widths_t = (2 * FOX_W, 2 * KV_W, 2 * KV_W, 2 * KV_W, LANES)
    hm = lambda w: pl.BlockSpec((NSA_HEADS, n, w), lambda i: (0, 0, 0))
    return pl.pallas_call(
        _in_proj_sample_kernel, grid=(1,), in_specs=in_specs,
        out_specs=[full((n, w)) for w in widths] + [hm(LANES), hm(2 * LANES)] + [full((w, n)) for w in widths_t],
        out_shape=[jax.ShapeDtypeStruct((n, w), F32) for w in widths] + [
            jax.ShapeDtypeStruct((NSA_HEADS, n, LANES), BF16), jax.ShapeDtypeStruct((NSA_HEADS, n, 2 * LANES), BF16)] + [
            jax.ShapeDtypeStruct((w, n), F32) for w in widths_t],
        compiler_params=_params(1),
        name="in_proj_sample",
    )(x.reshape(n, d), *ops, cos, sinm, sinp)


def _flash_step(q, k, v, m, acc, mask=None):
    s = _dot_t(q, k)
    if mask is not None:
        s = jnp.where(mask, s, NEG)
    m_new = jnp.maximum(m, jnp.max(s, axis=-1, keepdims=True))
    p = jnp.exp(s - m_new)
    acc = acc * jnp.exp(m - m_new) + _dot(p.astype(BF16), v)
    return m_new, acc


def _finish(acc):
    return acc / acc[:, HEAD_DIM:HEAD_DIM + 1]


def _pack_heads(a, b):
    return jnp.where(_lane(a.shape) < HEAD_DIM, a, pltpu.roll(b, HEAD_DIM, 1))


def _fox_prompt_kernel(q_ref, k_ref, v_ref, o_ref, *, tile):
    qi = pl.program_id(2)
    rowi = lax.broadcasted_iota(jnp.int32, (tile, tile), 0)
    coli = lax.broadcasted_iota(jnp.int32, (tile, tile), 1)
    outs = []
    for j in range(2):
        q = q_ref[0, j]

        def body(t, carry, j=j, q=q):
            off = pl.multiple_of(t * tile, tile)
            return _flash_step(q, k_ref[0, j, pl.ds(off, tile), :], v_ref[0, j, pl.ds(off, tile), :], *carry)

        init = (jnp.full((tile, 1), NEG, F32), jnp.zeros((tile, LANES), F32))
        m, acc = lax.fori_loop(0, qi, body, init)
        off = pl.multiple_of(qi * tile, tile)
        m, acc = _flash_step(q, k_ref[0, j, pl.ds(off, tile), :], v_ref[0, j, pl.ds(off, tile), :], m, acc,
                             mask=coli <= rowi)
        outs.append(_finish(acc))
    o_ref[0] = _pack_heads(outs[0], outs[1])


def _fox_prompt(fq, fk, fv, tile):
    b, h, s, _ = fq.shape
    qspec = pl.BlockSpec((1, 2, tile, LANES), lambda bi, hi, qi: (bi, hi, qi, 0))
    kspec = pl.BlockSpec((1, 2, s, LANES), lambda bi, hi, qi: (bi, hi, 0, 0))
    return pl.pallas_call(
        functools.partial(_fox_prompt_kernel, tile=tile),
        grid=(b, h // 2, s // tile), in_specs=[qspec, kspec, kspec],
        out_specs=pl.BlockSpec((1, tile, LANES), lambda bi, hi, qi: (bi, qi, hi)),
        out_shape=jax.ShapeDtypeStruct((b, s, h * HEAD_DIM), F32),
        compiler_params=_params(3),
        name="fox_prompt",
    )(fq, fk, fv)


def _compress_operands(pos_k, w_k1, w_k2, pos_v, w_v1, w_v2, g_k_cmp):
    half = CMP_BLOCK // 2
    wk = jnp.stack([w_k1, w_k1, w_v1, w_v1]).reshape(4, 2, half, HEAD_DIM, CMP_HIDDEN)
    wbig = jnp.einsum("kardh,kj->rkdajh", wk, jnp.eye(4, dtype=w_k1.dtype))
    wbig = wbig.reshape(half * 4 * HEAD_DIM, 2 * 4 * CMP_HIDDEN).astype(BF16)
    pos = jnp.stack([pos_k, pos_k, pos_v, pos_v]).reshape(4, 2, half, HEAD_DIM)
    pos_rows = jnp.transpose(pos, (1, 2, 0, 3)).reshape(2, half * 4 * HEAD_DIM)
    pos_rows = jnp.concatenate([pos_rows, jnp.zeros((6, pos_rows.shape[1]), pos_rows.dtype)]).astype(BF16)
    pad2 = lambda w: jnp.pad(w, ((0, 0), (0, LANES - HEAD_DIM))).astype(BF16)
    gain = jnp.pad(g_k_cmp, (0, LANES - HEAD_DIM)).reshape(1, LANES)
    return wbig, pos_rows, pad2(w_k2), pad2(w_v2), gain


def _compress_core(x, w_ref, pos_ref, w2k_ref, w2v_ref, gain_ref, kc_ref, vc_ref):
    r = x.shape[0]
    hw = 4 * CMP_HIDDEN
    ab = _dot(x.astype(BF16), w_ref[...])
    pw = _dot(pos_ref[...], w_ref[...])
    hid = ab[:, :hw] + pltpu.roll(ab[:, hw:], r - 1, 0) + pw[0:1, :hw] + pw[1:2, hw:]
    hid = (hid * (1.0 / (1.0 + jnp.exp(-hid)))).astype(BF16)
    lane = _lane((r, LANES))
    for kvh in range(NSA_KV_HEADS):
        kk = _dot(hid[:, kvh * CMP_HIDDEN:(kvh + 1) * CMP_HIDDEN], w2k_ref[...])
        ms = jnp.sum(kk * kk, axis=-1, keepdims=True) * (1.0 / HEAD_DIM)
        kc_ref[0, kvh] = (kk * lax.rsqrt(ms + EPS) * gain_ref[...]).astype(BF16)
        vv = _dot(hid[:, (2 + kvh) * CMP_HIDDEN:(3 + kvh) * CMP_HIDDEN], w2v_ref[...])
        vc_ref[0, kvh] = jnp.where(lane == HEAD_DIM, 1.0, vv).astype(BF16)


def _compress_prompt_kernel(x_ref, w_ref, pos_ref, w2k_ref, w2v_ref, gain_ref, kc_ref, vc_ref):
    _compress_core(x_ref[0], w_ref, pos_ref, w2k_ref, w2v_ref, gain_ref, kc_ref, vc_ref)


def _compress_prompt(ckv, cops, b):
    n = ckv.shape[0]
    r = n // b // CMP_STRIDE
    width = CMP_STRIDE * ckv.shape[1]
    out = jax.ShapeDtypeStruct((b, NSA_KV_HEADS, r, LANES), BF16)
    ospec = pl.BlockSpec((1, NSA_KV_HEADS, r, LANES), lambda i: (i, 0, 0, 0))
    return pl.pallas_call(
        _compress_prompt_kernel, grid=(b,),
        in_specs=[pl.BlockSpec((1, r, width), lambda i: (i, 0, 0))] + [_const_spec(o.shape) for o in cops],
        out_specs=[ospec, ospec], out_shape=[out, out],
        compiler_params=_params(1),
        name="compress_prompt",
    )(ckv.reshape(b, r, width), *cops)


def _cmp_branch(q, kc, vc, qpos_rows, n_valid_limit):
    s = _dot_t(q, kc)
    cidx = _lane(s.shape)
    valid = (cidx * CMP_STRIDE + (CMP_BLOCK - 1) <= qpos_rows) & (cidx < n_valid_limit)
    s = jnp.where(valid, s, NEG)
    p = jnp.exp(s - jnp.max(s, axis=-1, keepdims=True)) * valid.astype(F32)
    l = jnp.sum(p, axis=-1, keepdims=True)
    p = p / jnp.where(l > 0, l, 1.0)
    return _dot(p.astype(BF16), vc), p


def _select_blocks(imp, qblk, n_sel):
    j = _lane(imp.shape)
    dist = qblk - j
    forced = (j == 0) | ((dist >= 0) & (dist < N_LOCAL))
    score = jnp.where(dist >= 0, jnp.where(forced, FORCE, imp), NEG)
    score = jnp.where(j < n_sel, score, -jnp.inf)
    cnt = jnp.zeros(imp.shape, F32)
    for i in range(n_sel):
        col = score[:, i:i + 1]
        beats = (col > score) | ((col == score) & (j > i))
        cnt = cnt + beats.astype(F32)
    return (cnt < min(TOP_N, n_sel)) & (j < n_sel)


def _nsa_prompt_kernel(q_ref, kc_ref, vc_ref, ks_ref, vs_ref, kw_ref, vw_ref, sm_ref, msel_ref, o_ref, *, tq, n_cmp):
    i = pl.program_id(1)
    start = i * tq
    g = NSA_GROUP
    rows = g * tq
    n_sel = msel_ref.shape[1]
    qpos = start + lax.broadcasted_iota(jnp.int32, (tq, 1), 0)
    qpos_rows = jnp.concatenate([qpos] * g, axis=0)
    small = sm_ref[...]
    kpos_l = lax.broadcasted_iota(jnp.int32, (tq, tq), 1)
    blk_of_lane = lax.broadcasted_iota(jnp.int32, (n_sel, tq), 1) // SEL_BLOCK
    blk_row = lax.broadcasted_iota(jnp.int32, (n_sel, tq), 0)
    per = tq // SEL_BLOCK
    init = (jnp.full((rows, 1), NEG, F32), jnp.zeros((rows, LANES), F32))
    heads = []
    for k in range(NSA_KV_HEADS):
        q = q_ref[0, k * g:(k + 1) * g].reshape(rows, LANES)
        o_c, p = _cmp_branch(q, kc_ref[0, k], vc_ref[0, k], qpos_rows, n_cmp)
        psum = p[0:tq]
        for gi in range(1, g):
            psum = psum + p[gi * tq:(gi + 1) * tq]
        hi = psum.astype(BF16)
        lo = (psum - hi.astype(F32)).astype(BF16)
        imp = _dot(hi, msel_ref[...]) + _dot(lo, msel_ref[...])
        selb = _select_blocks(imp, qpos // SEL_BLOCK, n_sel).astype(BF16)

        def sel_body(t, carry, k=k, q=q, selb=selb):
            off = pl.multiple_of(t * tq, tq)
            expand = (blk_row == t * per + blk_of_lane).astype(BF16)
            picked = _dot(selb, expand) > 0.5
            ok = picked & (t * tq + kpos_l <= qpos)
            s = _dot_t(q, ks_ref[0, k, pl.ds(off, tq), :]).reshape(g, tq, tq)
            s = jnp.where(ok[None], s, NEG).reshape(rows, tq)
            m, acc = carry
            m_new = jnp.maximum(m, jnp.max(s, axis=-1, keepdims=True))
            pr = jnp.exp(s - m_new)
            acc = acc * jnp.exp(m - m_new) + _dot(pr.astype(BF16), vs_ref[0, k, pl.ds(off, tq), :])
            return m_new, acc

        _, acc_s = lax.fori_loop(0, i + 1, sel_body, init)

        def win_body(t, carry, k=k, q=q):
            off = pl.multiple_of(t * tq, tq)
            diff = qpos - (t * tq + kpos_l)
            ok = (diff >= 0) & (diff < WINDOW)
            s = _dot_t(q, kw_ref[0, k, pl.ds(off, tq), :]).reshape(g, tq, tq)
            s = jnp.where(ok[None], s, NEG).reshape(rows, tq)
            m, acc = carry
            m_new = jnp.maximum(m, jnp.max(s, axis=-1, keepdims=True))
            pr = jnp.exp(s - m_new)
            acc = acc * jnp.exp(m - m_new) + _dot(pr.astype(BF16), vw_ref[0, k, pl.ds(off, tq), :])
            return m_new, acc

        n_back = -(-WINDOW // tq)
        _, acc_w = lax.fori_loop(jnp.maximum(i - n_back, 0), i + 1, win_body, init)
        o_s, o_w = _finish(acc_s), _finish(acc_w)
        for gi in range(g):
            h = k * g + gi
            sl = slice(gi * tq, (gi + 1) * tq)
            c0 = SM_GATE + 3 * h
            heads.append(small[:, c0:c0 + 1] * o_c[sl] + small[:, c0 + 1:c0 + 2] * o_s[sl] + small[:, c0 + 2:c0 + 3] * o_w[sl])
    for c in range(NSA_HEADS // 2):
        o_ref[0, :, c * LANES:(c + 1) * LANES] = _pack_heads(heads[2 * c], heads[2 * c + 1])


def _cmp_to_sel_matrix(n_rows, n_sel):
    lead = CMP_BLOCK // CMP_STRIDE - 1
    r = SEL_BLOCK // CMP_STRIDE
    c = np.arange(n_rows)[:, None]
    j = np.arange(n_sel)[None, :]
    return jnp.asarray((c >= r * j - lead) & (c <= r * j + r - 1), BF16)


def _nsa_prompt(nq, kc, vc, ksel, vsel, kwin, vwin, small, tq):
    b, h, s, _ = nq.shape
    nb = s // tq
    r = kc.shape[2]
    n_cmp = (s - CMP_BLOCK) // CMP_STRIDE + 1
    msel = _cmp_to_sel_matrix(r, s // SEL_BLOCK)
    seq_spec = lambda rows: pl.BlockSpec((1, NSA_KV_HEADS, rows, LANES), lambda bi, qi: (bi, 0, 0, 0))
    return pl.pallas_call(
        functools.partial(_nsa_prompt_kernel, tq=tq, n_cmp=n_cmp),
        grid=(b, nb),
        in_specs=[pl.BlockSpec((1, h, tq, LANES), lambda bi, qi: (bi, 0, qi, 0)), seq_spec(r), seq_spec(r),
                  seq_spec(s), seq_spec(s), seq_spec(s), seq_spec(s),
                  pl.BlockSpec((tq, LANES), lambda bi, qi: (bi * nb + qi, 0)), _const_spec(msel.shape)],
        out_specs=pl.BlockSpec((1, tq, h * HEAD_DIM), lambda bi, qi: (bi, qi, 0)),
        out_shape=jax.ShapeDtypeStruct((b, s, h * HEAD_DIM), F32),
        compiler_params=_params(2),
        name="nsa_prompt",
    )(nq, kc, vc, ksel, vsel, kwin, vwin, small, msel)


def _rms(x, gain):
    return x * lax.rsqrt(jnp.mean(x * x, axis=-1, keepdims=True) + EPS) * gain


def _first_lane_where(cond, lane):
    return jnp.min(jnp.where(cond, lane, LANES), axis=-1, keepdims=True)


def _post_attn_kernel(of_ref, on_ref, x_ref, gf_ref, gn_ref, wo_ref, gffn_ref, wr_hi_ref, wr_lo_ref, br_ref,
                      x1_ref, h_ref, gate_ref):
    nf = _rms(of_ref[...], gf_ref[...]).astype(BF16)
    nn = _rms(on_ref[...], gn_ref[...]).astype(BF16)
    x1 = x_ref[...] + _dot(nf, wo_ref[0:FOX_W, :]) + _dot(nn, wo_ref[FOX_W:FOX_W + NSA_W, :])
    x1_ref[...] = x1
    h = _rms(x1, gffn_ref[...])
    h_ref[...] = h.astype(BF16)
    h_hi = h.astype(BF16)
    h_lo = (h - h_hi.astype(F32)).astype(BF16)
    logits = _dot(h_hi, wr_hi_ref[...]) + _dot(h_lo, wr_hi_ref[...]) + _dot(h_hi, wr_lo_ref[...]) + br_ref[...]
    lane = _lane(logits.shape)
    is_grp = (lane >= N_EXPERTS) & (lane < N_EXPERTS + N_GROUPS)
    lg = jnp.where(is_grp, logits, NEG)
    lg_max = jnp.max(lg, axis=-1, keepdims=True)
    gsel = _first_lane_where(is_grp & (lg == lg_max), lane) - N_EXPERTS
    pg_sel = 1.0 / jnp.sum(jnp.where(is_grp, jnp.exp(lg - lg_max), 0.0), axis=-1, keepdims=True)
    in_grp = (lane >= gsel * EXPERTS_PER_GROUP) & (lane < (gsel + 1) * EXPERTS_PER_GROUP)
    le = jnp.where(in_grp, logits, NEG)
    pe = jnp.where(in_grp, jnp.exp(le - jnp.max(le, axis=-1, keepdims=True)), 0.0)
    pe = pe / jnp.sum(pe, axis=-1, keepdims=True)
    pe = jnp.where(in_grp, pe, -1.0)
    v1 = jnp.max(pe, axis=-1, keepdims=True)
    i1 = _first_lane_where(pe == v1, lane)
    rest = jnp.where(lane == i1, -1.0, pe)
    v2 = jnp.max(rest, axis=-1, keepdims=True)
    i2 = _first_lane_where(rest == v2, lane)
    tot = v1 + v2
    gate_ref[...] = jnp.where(lane == i1, v1 / tot * pg_sel, jnp.where(lane == i2, v2 / tot * pg_sel, 0.0))


def _moe_kernel(h_ref, gate_ref, x1_ref, wgu_ref, wd_ref, y_ref):
    e = pl.program_id(1)

    @pl.when(e == 0)
    def _():
        y_ref[...] = x1_ref[...]

    d_exp = wd_ref.shape[1]
    gu = _dot(h_ref[...], wgu_ref[0])
    gpre, up = gu[:, :d_exp], gu[:, d_exp:]
    gate = gate_ref[...]
    ge = jnp.sum(jnp.where(_lane(gate.shape) == e, gate, 0.0), axis=-1, keepdims=True)
    hid = gpre * (1.0 / (1.0 + jnp.exp(-gpre))) * up * ge
    y_ref[...] += _dot(hid.astype(BF16), wd_ref[0])


def _finish_operands(g_out_fox, g_out_nsa, w_out, g_ffn, w_rg, b_rg, w_re, b_re, w_gate, w_up, w_down):
    d = w_out.shape[1]
    pad = LANES - N_EXPERTS - N_GROUPS
    wr = jnp.concatenate([w_re, w_rg, jnp.zeros((d, pad), F32)], axis=1)
    wr_hi = wr.astype(BF16)
    wr_lo = (wr - wr_hi.astype(F32)).astype(BF16)
    br = jnp.concatenate([b_re, b_rg, jnp.zeros((pad,), F32)]).reshape(1, LANES)
    post = (g_out_fox.reshape(1, -1), g_out_nsa.reshape(1, -1), w_out.astype(BF16), g_ffn.reshape(1, d), wr_hi, wr_lo, br)
    moe = (jnp.concatenate([w_gate, w_up], axis=-1).astype(BF16), w_down.astype(BF16))
    return post, moe


def _finish_layer(o_fox, o_nsa, x, post_ops, moe_ops, tm):
    n, d = x.shape
    tp = _row_tile(n, 512)
    row = lambda w: pl.BlockSpec((tp, w), lambda i: (i, 0))
    x1, h, gate = pl.pallas_call(
        _post_attn_kernel, grid=(n // tp,),
        in_specs=[row(FOX_W), row(NSA_W), row(d)] + [_const_spec(o.shape) for o in post_ops],
        out_specs=[row(d), row(d), row(LANES)],
        out_shape=[jax.ShapeDtypeStruct((n, d), F32), jax.ShapeDtypeStruct((n, d), BF16), jax.ShapeDtypeStruct((n, LANES), F32)],
        compiler_params=_params(1),
        name="post_attn",
    )(o_fox, o_nsa, x, *post_ops)
    wgu, wd = moe_ops
    tok = lambda w: pl.BlockSpec((tm, w), lambda i, e: (i, 0))
    return pl.pallas_call(
        _moe_kernel, grid=(n // tm, wgu.shape[0]),
        in_specs=[tok(d), tok(LANES), tok(d),
                  pl.BlockSpec((1,) + wgu.shape[1:], lambda i, e: (e, 0, 0)),
                  pl.BlockSpec((1,) + wd.shape[1:], lambda i, e: (e, 0, 0))],
        out_specs=tok(d), out_shape=jax.ShapeDtypeStruct((n, d), F32),
        compiler_params=_params(2),
        name="moe",
    )(h, gate, x1, wgu, wd)


def _row_tile(n, cap):
    for t in range(min(cap, n), 7, -1):
        if n % t == 0 and t % 8 == 0:
            return t
    return n


def _logf_suffix_kernel(lf_ref, u_ref, ones_ref, w_ref, tot_ref):
    hi, mid, lo = _split3(lf_ref[...])
    u, o = u_ref[...], ones_ref[...]
    w_ref[...] = _dot(hi, u) + _dot(mid, u) + _dot(lo, u)
    tot_ref[...] = _dot(hi, o) + _dot(mid, o) + _dot(lo, o)


def _logf_suffix(logf_t):
    n_phys, nh, page = logf_t.shape
    rows = n_phys * nh
    t = np.arange(page)
    u = jnp.asarray(t[:, None] > t[None, :], BF16)
    ones = jnp.ones((page, page), BF16)
    tr = _row_tile(rows, 2048)
    spec = pl.BlockSpec((tr, page), lambda i: (i, 0))
    out = jax.ShapeDtypeStruct((rows, page), F32)
    w, tot = pl.pallas_call(
        _logf_suffix_kernel, grid=(rows // tr,),
        in_specs=[spec, _const_spec(u.shape), _const_spec(ones.shape)],
        out_specs=[spec, spec], out_shape=[out, out],
        compiler_params=_params(1),
        name="logf_suffix",
    )(logf_t.reshape(rows, page), u, ones)
    return w.reshape(n_phys, nh, page), tot.reshape(n_phys, nh, page)


def _per_head(col):
    nh, w = col.shape
    return jnp.broadcast_to(col[:, None, :], (nh, HEAD_DIM, w)).reshape(nh * HEAD_DIM, w)


def _head_sums(x):
    return jnp.sum(x.reshape(x.shape[0] // HEAD_DIM, HEAD_DIM, x.shape[1]), axis=1)


def _fox_sample_kernel(pt_ref, q_ref, new_ref, lfn_ref, *rest, pc):
    del pt_ref
    kv_refs, w_refs, tot_refs = rest[:pc], rest[pc:2 * pc], rest[2 * pc:3 * pc]
    o_ref = rest[3 * pc]
    qb_ref, m_ref, carry_ref, acc_ref, l_ref = rest[3 * pc + 1:]
    c = pl.program_id(1)
    page = acc_ref.shape[1]

    @pl.when(c == 0)
    def _():
        qb_ref[...] = jnp.broadcast_to(q_ref[0], qb_ref.shape)
        m_ref[...] = jnp.full(m_ref.shape, NEG, F32)
        carry_ref[...] = jnp.zeros_like(carry_ref)
        acc_ref[...] = jnp.zeros_like(acc_ref)
        l_ref[...] = jnp.zeros_like(l_ref)

    qb = qb_ref[...]
    carry = carry_ref[...]
    scores = []
    for j in range(pc):
        scores.append(_head_sums(kv_refs[j][0, 0] * qb) + (lfn_ref[0] + carry + w_refs[j][0]))
        carry = carry + tot_refs[j][0]
    carry_ref[...] = carry
    m_chunk = scores[0]
    for s in scores[1:]:
        m_chunk = jnp.maximum(m_chunk, s)
    m_old = m_ref[...]
    m_new = jnp.maximum(m_old, jnp.max(m_chunk, axis=-1, keepdims=True))
    m_ref[...] = m_new
    alpha = jnp.exp(m_old - m_new)
    acc = acc_ref[...] * _per_head(alpha)
    l = l_ref[...] * alpha
    for j in range(pc):
        p = jnp.exp(scores[j] - m_new)
        l = l + p
        acc = acc + _per_head(p) * kv_refs[j][0, 1]
    acc_ref[...] = acc
    l_ref[...] = l

    @pl.when(c == pl.num_programs(1) - 1)
    def _():
        width = qb.shape[0]
        k_new, v_new = new_ref[0, 0:width, :], new_ref[0, width:2 * width, :]
        s_new = _head_sums(k_new * q_ref[0])
        m_fin = jnp.maximum(m_new, s_new)
        a = jnp.exp(m_new - m_fin)
        p_new = jnp.exp(s_new - m_fin)
        l_tot = jnp.sum(l, axis=-1, keepdims=True) * a + p_new
        acc_col = jnp.sum(acc, axis=-1, keepdims=True) * _per_head(a) + _per_head(p_new) * v_new
        o_ref[0] = acc_col / _per_head(l_tot)


def _fox_sample(fq, fkv_new, lf_new, kv_t, w_suffix, tot, page_table, pc):
    db = fq.shape[0]
    n_phys, _, width, page = kv_t.shape
    nh = w_suffix.shape[1]
    n_pages = page_table.shape[1]
    phys = lambda s, c, pt, j: pt[jnp.minimum(s, db - 1), jnp.clip(n_pages - 1 - (c * pc + j), 0, n_pages - 1)]
    per_seq = lambda r: pl.BlockSpec((1, r, 1), lambda s, c, pt: (s, 0, 0))
    in_specs = [per_seq(width), per_seq(2 * width), per_seq(nh)]
    in_specs += [pl.BlockSpec((1, 2, width, page), lambda s, c, pt, j=j: (phys(s, c, pt, j), 0, 0, 0)) for j in range(pc)]
    in_specs += [pl.BlockSpec((1, nh, page), lambda s, c, pt, j=j: (phys(s, c, pt, j), 0, 0)) for j in list(range(pc)) * 2]
    out = pl.pallas_call(
        functools.partial(_fox_sample_kernel, pc=pc),
        grid_spec=pltpu.PrefetchScalarGridSpec(
            num_scalar_prefetch=1, grid=(db, n_pages // pc), in_specs=in_specs,
            out_specs=pl.BlockSpec((1, width, 1), lambda s, c, pt: (s, 0, 0)),
            scratch_shapes=[pltpu.VMEM((width, page), F32), pltpu.VMEM((nh, 1), F32), pltpu.VMEM((nh, page), F32),
                            pltpu.VMEM((width, page), F32), pltpu.VMEM((nh, page), F32)]),
        out_shape=jax.ShapeDtypeStruct((db, width, 1), F32),
        compiler_params=_params(2),
        name="fox_sample",
    )(page_table, fq.reshape(db, width, 1), fkv_new.reshape(db, 2 * width, 1), lf_new.reshape(db, nh, 1),
      *([kv_t] * pc), *([w_suffix] * pc), *([tot] * pc))
    return out.reshape(db, width)


def _nsa_sample_cmp_kernel(pt_ref, q_ref, wbig_ref, posr_ref, w2k_ref, w2v_ref, gain_ref, msel_ref, utri_ref, *rest,
                           n_pages, past, n_cmp, n_sel):
    del pt_ref
    pages = rest[:n_pages]
    oc_ref, idx_ref = rest[n_pages:n_pages + 2]
    x_ref, kc_ref, vc_ref = rest[n_pages + 2:]
    rpp = pages[0].shape[1]
    for j in range(n_pages):
        x_ref[j * rpp:(j + 1) * rpp, :] = pages[j][0]
    _compress_core(x_ref[...], wbig_ref, posr_ref, w2k_ref, w2v_ref, gain_ref, kc_ref, vc_ref)
    q8 = q_ref[0]
    row = lax.broadcasted_iota(jnp.int32, (NSA_HEADS, 1), 0)
    lane_row = _lane((1, LANES))
    oc = jnp.zeros((NSA_HEADS, LANES), F32)
    idx_row = jnp.zeros((1, LANES), jnp.int32)
    for k in range(NSA_KV_HEADS):
        o_k, p = _cmp_branch(q8, kc_ref[0, k], vc_ref[0, k], past, n_cmp)
        in_grp = row // NSA_GROUP == k
        oc = jnp.where(in_grp, o_k, oc)
        psum = jnp.broadcast_to(jnp.sum(jnp.where(in_grp, p, 0.0), axis=0, keepdims=True), p.shape)
        hi = psum.astype(BF16)
        lo = (psum - hi.astype(F32)).astype(BF16)
        imp = _dot(hi, msel_ref[...]) + _dot(lo, msel_ref[...])
        sel = _select_blocks(imp, past // SEL_BLOCK, n_sel)
        rank = _dot(sel.astype(BF16), utri_ref[...])
        j = _lane(imp.shape)
        for n in range(TOP_N):
            val = jnp.sum(jnp.where(sel & (rank == n), j, 0), axis=-1, keepdims=True)
            idx_row = jnp.where(lane_row == k * TOP_N + n, val[0:1], idx_row)
    oc_ref[0] = oc
    idx_ref[0] = idx_row


def _nsa_sample_cmp(nqh, cache_cmp, page_table, cops, past):
    db = nqh.shape[0]
    n_phys, page = cache_cmp.shape[0], cache_cmp.shape[1]
    n_pages = page_table.shape[1]
    rpp = page // CMP_STRIDE
    width = CMP_STRIDE * 2 * KV_W
    r = n_pages * rpp
    n_cmp = (past + 1 - CMP_BLOCK) // CMP_STRIDE + 1
    n_sel = past // SEL_BLOCK + 1
    sel_pad = -(-n_sel // LANES) * LANES
    msel = _cmp_to_sel_matrix(r, sel_pad)
    i = np.arange(sel_pad)
    utri = jnp.asarray(i[:, None] < i[None, :], BF16)
    cst = lambda a: pl.BlockSpec(a.shape, lambda s, pt: (0,) * a.ndim)
    in_specs = [pl.BlockSpec((1, NSA_HEADS, LANES), lambda s, pt: (s, 0, 0))] + [cst(a) for a in cops] + [cst(msel), cst(utri)]
    in_specs += [pl.BlockSpec((1, rpp, width), lambda s, pt, j=j: (pt[jnp.minimum(s, db - 1), j], 0, 0))
                 for j in range(n_pages)]
    o_c, idx = pl.pallas_call(
        functools.partial(_nsa_sample_cmp_kernel, n_pages=n_pages, past=past, n_cmp=n_cmp, n_sel=n_sel),
        grid_spec=pltpu.PrefetchScalarGridSpec(
            num_scalar_prefetch=1, grid=(db,), in_specs=in_specs,
            out_specs=[pl.BlockSpec((1, NSA_HEADS, LANES), lambda s, pt: (s, 0, 0)),
                       pl.BlockSpec((1, 1, LANES), lambda s, pt: (s, 0, 0))],
            scratch_shapes=[pltpu.VMEM((r, width), F32), pltpu.VMEM((1, NSA_KV_HEADS, r, LANES), BF16),
                            pltpu.VMEM((1, NSA_KV_HEADS, r, LANES), BF16)]),
        out_shape=[jax.ShapeDtypeStruct((db, NSA_HEADS, LANES), F32), jax.ShapeDtypeStruct((db, 1, LANES), jnp.int32)],
        compiler_params=_params(1),
        name="nsa_sample_cmp",
    )(page_table, nqh, *cops, msel, utri, *([cache_cmp.reshape(n_phys, rpp, width)] * n_pages))
    return o_c, idx[:, 0, :NSA_KV_HEADS * TOP_N]


def _nsa_sample_attend_kernel(pt_ref, idx_ref, qmat_ref, oc_ref, sm_ref, snew_ref, snewc_ref, wnew_ref, wnewc_ref,
                              state_ref, *rest, nb_past, bpp):
    del pt_ref
    n_blk = NSA_KV_HEADS * TOP_N
    pages = rest[:n_blk]
    o_ref, wout_ref = rest[n_blk:]
    s = pl.program_id(0)
    g = NSA_GROUP
    qm = qmat_ref[0]
    feat = qm.shape[1]
    row = lax.broadcasted_iota(jnp.int32, (NSA_HEADS, 1), 0)

    def new_score(col_ref):
        return _dot(qm, jnp.broadcast_to(col_ref[0], (feat, LANES)).astype(BF16))[:, 0:1]

    lane_p = _lane((NSA_HEADS, pages[0].shape[2]))
    tiles = []
    has_new = jnp.zeros((NSA_HEADS, 1), jnp.bool_)
    for k in range(NSA_KV_HEADS):
        in_grp = row // g == k
        for n in range(TOP_N):
            idx = idx_ref[s, k * TOP_N + n]
            is_new = idx >= nb_past
            half = jnp.minimum(idx, nb_past - 1) % bpp
            data = pages[k * TOP_N + n][0].astype(BF16)
            ok = in_grp & jnp.logical_not(is_new) & (lane_p // SEL_BLOCK == half)
            tiles.append((jnp.where(ok, _dot(qm, data), NEG), data))
            has_new = has_new | (in_grp & is_new)
    s_new = jnp.where(has_new, new_score(snewc_ref), NEG)
    m = s_new
    for sc, _ in tiles:
        m = jnp.maximum(m, sc.max(axis=-1, keepdims=True))
    l = jnp.exp(s_new - m)
    acc = l * snew_ref[0]
    for sc, data in tiles:
        p = jnp.exp(sc - m)
        l = l + jnp.sum(p, axis=-1, keepdims=True)
        acc = acc + _dot_t(p.astype(BF16), data)
    o_s = acc / l
    st = state_ref[0]
    w_keep = st.shape[1]
    stb = st.astype(BF16)
    sw = _dot(qm, stb)
    sw = jnp.where(_lane(sw.shape) >= w_keep - WINDOW + 1, sw, NEG)
    s_new = new_score(wnewc_ref)
    m = jnp.maximum(jnp.max(sw, axis=-1, keepdims=True), s_new)
    p = jnp.exp(sw - m)
    p_new = jnp.exp(s_new - m)
    o_w = (_dot_t(p.astype(BF16), stb) + p_new * wnew_ref[0]) / (jnp.sum(p, axis=-1, keepdims=True) + p_new)

    def values_of(a):
        v = a[:, LANES:2 * LANES]
        return jnp.where(row < g, v, pltpu.roll(v, HEAD_DIM, 1))

    smb = jnp.broadcast_to(sm_ref[0], (NSA_HEADS, LANES))
    lane = _lane(smb.shape)
    gate = lambda j: jnp.sum(jnp.where(lane == SM_GATE + 3 * row + j, smb, 0.0), axis=-1, keepdims=True)
    o_ref[0] = gate(0) * oc_ref[0] + gate(1) * values_of(o_s) + gate(2) * values_of(o_w)
    lane_f = _lane((feat, LANES))
    n_chunks = w_keep // LANES
    shifted = [pltpu.roll(st[:, c * LANES:(c + 1) * LANES], LANES - 1, 1) for c in range(n_chunks)]
    shifted.append(jnp.broadcast_to(wnewc_ref[0], (feat, LANES)))
    for c in range(n_chunks):
        wout_ref[0, :, c * LANES:(c + 1) * LANES] = jnp.where(lane_f < LANES - 1, shifted[c], shifted[c + 1])


def _nsa_sample_attend(qmat, o_c, small, skv_new, wkv_new, state_t, sel_t, page_table, sel_idx):
    db = qmat.shape[0]
    n_phys, feat, page = sel_t.shape
    bpp = page // SEL_BLOCK
    nb_past = page_table.shape[1] * bpp
    w_keep = state_t.shape[2]
    assert w_keep % LANES == 0
    n_blk = NSA_KV_HEADS * TOP_N

    def page_map(s, pt, idx, n):
        row = jnp.minimum(s, db - 1)
        return (pt[row, jnp.clip(idx[row, n], 0, nb_past - 1) // bpp], 0, 0)

    per_seq = lambda a, b: pl.BlockSpec((1, a, b), lambda s, pt, idx: (s, 0, 0))
    in_specs = [per_seq(NSA_HEADS, feat), per_seq(NSA_HEADS, LANES), per_seq(1, LANES), per_seq(1, feat), per_seq(feat, 1),
                per_seq(1, feat), per_seq(feat, 1), per_seq(feat, w_keep)]
    in_specs += [pl.BlockSpec((1, feat, page), functools.partial(page_map, n=n)) for n in range(n_blk)]
    return pl.pallas_call(
        functools.partial(_nsa_sample_attend_kernel, nb_past=nb_past, bpp=bpp),
        grid_spec=pltpu.PrefetchScalarGridSpec(
            num_scalar_prefetch=2, grid=(db,), in_specs=in_specs,
            out_specs=[per_seq(NSA_HEADS, LANES), per_seq(feat, w_keep)]),
        out_shape=[jax.ShapeDtypeStruct((db, NSA_HEADS, LANES), F32), jax.ShapeDtypeStruct((db, feat, w_keep), F32)],
        compiler_params=_params(1),
        name="nsa_sample_attend",
    )(page_table, sel_idx, qmat, o_c, small.reshape(db, 1, LANES), skv_new.reshape(db, 1, feat), skv_new.reshape(db, feat, 1),
      wkv_new.reshape(db, 1, feat), wkv_new.reshape(db, feat, 1), state_t, *([sel_t] * n_blk))


def kernel(x_prompt, x_sample, cache_fox_kv, cache_fox_logf, cache_cmp_kv, cache_sel_kv, state_win_kv, page_table, g_attn, w_in, b_forget, b_gate, g_q_fox, g_k_fox, g_q_nsa, g_k_cmp, g_k_sel, g_k_win, pos_cmp_k, w_cmp_k1, w_cmp_k2, pos_cmp_v, w_cmp_v1, w_cmp_v2, g_out_fox, g_out_nsa, w_out, g_ffn, w_router_group, b_router_group, w_router_expert, b_router_expert, w_exp_gate, w_exp_up, w_exp_down):
    assert w_in.shape[0] == 1 and x_sample.shape[1] == 1
    l = 0
    proj_ops = _in_proj_operands(g_attn[l], w_in[l], b_forget[l], b_gate[l], g_q_fox[l], g_k_fox[l], g_q_nsa[l],
                                 g_k_sel[l], g_k_win[l])
    cops = _compress_operands(pos_cmp_k[l], w_cmp_k1[l], w_cmp_k2[l], pos_cmp_v[l], w_cmp_v1[l], w_cmp_v2[l], g_k_cmp[l])
    post_ops, moe_ops = _finish_operands(g_out_fox[l], g_out_nsa[l], w_out[l], g_ffn[l], w_router_group[l],
                                         b_router_group[l], w_router_expert[l], b_router_expert[l], w_exp_gate[l],
                                         w_exp_up[l], w_exp_down[l])
    y_p, fkv_p, lf_p, ckv_p, skv_p, wkv_p = _prompt_group(x_prompt, proj_ops, cops, post_ops, moe_ops)
    y_s, fkv_s, lf_s, ckv_s, skv_s, wkv_s = _sample_group(x_sample, cache_fox_kv[l], cache_fox_logf[l], cache_cmp_kv[l],
                                                          cache_sel_kv[l], state_win_kv[l], page_table, proj_ops, cops,
                                                          post_ops, moe_ops)
    return (y_p, y_s, fkv_p, fkv_s, lf_p, lf_s, ckv_p, ckv_s, skv_p, skv_s, wkv_p, wkv_s)


def _prompt_group(x, proj_ops, cops, post_ops, moe_ops):
    b, s, d = x.shape
    tile = _row_tile(s, 512)
    (fkv, ckv, skv, wkv, small, fq, fk, fv, nq, ksel, vsel, kwin, vwin) = _in_proj_prompt(x, proj_ops, tile)
    o_fox = _fox_prompt(fq, fk, fv, tile)
    kc, vc = _compress_prompt(ckv, cops, b)
    o_nsa = _nsa_prompt(nq, kc, vc, ksel, vsel, kwin, vwin, small, 128)
    y = _finish_layer(o_fox.reshape(b * s, FOX_W), o_nsa.reshape(b * s, NSA_W), x.reshape(b * s, d), post_ops, moe_ops,
                      _row_tile(b * s, 1024))
    win = min(WINDOW, s)
    kvh, hd = NSA_KV_HEADS, HEAD_DIM
    return (y.reshape(b, s, d), fkv.reshape(1, b, s, 2, FOX_HEADS, hd),
            small[:, SM_LOGF:SM_LOGF + FOX_HEADS].reshape(1, b, s, FOX_HEADS),
            ckv.reshape(1, b, s, 2, kvh, hd), skv.reshape(1, b, s, 2, kvh, hd),
            wkv.reshape(b, s, 2, kvh, hd)[:, s - win:].reshape(1, b, win, 2, kvh, hd))


def _sample_group(x, cache_fox_kv, cache_fox_logf, cache_cmp_kv, cache_sel_kv, state_win_kv, page_table, proj_ops, cops,
                  post_ops, moe_ops):
    db, _, d = x.shape
    n_phys, page = cache_fox_kv.shape[0], cache_fox_kv.shape[1]
    n_pages = page_table.shape[1]
    past = n_pages * page
    w_keep = state_win_kv.shape[1]
    kvh, hd = NSA_KV_HEADS, HEAD_DIM
    fox_t = jnp.transpose(cache_fox_kv, (0, 2, 3, 4, 1)).reshape(n_phys, 2, FOX_W, page)
    logf_t = jnp.transpose(cache_fox_logf, (0, 2, 1))
    sel_t = jnp.transpose(cache_sel_kv, (0, 2, 3, 4, 1)).reshape(n_phys, 2 * KV_W, page)
    state_t = jnp.transpose(state_win_kv, (0, 2, 3, 4, 1)).reshape(db, 2 * KV_W, w_keep)
    (fkv, skv, wkv, small, fq, nqh, qmat, fkv_t, ckv_t, skv_t, wkv_t, small_t) = _in_proj_sample(x, proj_ops, past)
    w_suffix, tot = _logf_suffix(logf_t)
    pages_per_step = max(t for t in range(1, 17) if n_pages % t == 0)
    o_fox = _fox_sample(fq, fkv, small[:, SM_LOGF:SM_LOGF + FOX_HEADS], fox_t, w_suffix, tot, page_table, pages_per_step)
    o_c, sel_idx = _nsa_sample_cmp(jnp.transpose(nqh, (1, 0, 2)), cache_cmp_kv.reshape(n_phys, page, 2 * KV_W), page_table,
                                   cops, past)
    o_nsa, win_t = _nsa_sample_attend(jnp.transpose(qmat, (1, 0, 2)), o_c, small, skv, wkv, state_t, sel_t, page_table,
                                      sel_idx)
    y = _finish_layer(o_fox, o_nsa[:, :, :HEAD_DIM].reshape(db, NSA_W), x.reshape(db, d), post_ops, moe_ops,
                      _row_tile(db, 1024))
    rows = lambda a_t, *dims: jnp.transpose(a_t.reshape(*dims, db), (len(dims),) + tuple(range(len(dims)))).reshape(
        (1, db, 1) + dims)
    return (y.reshape(db, 1, d), rows(fkv_t, 2, FOX_HEADS, hd), rows(small_t[SM_LOGF:SM_LOGF + FOX_HEADS], FOX_HEADS),
            rows(ckv_t, 2, kvh, hd), rows(skv_t, 2, kvh, hd),
            jnp.transpose(win_t.reshape(db, 2, kvh, hd, w_keep), (0, 4, 1, 2, 3))[None])
```

```python
import functools

import jax
import jax.numpy as jnp
import numpy as np
from jax import lax
from jax.experimental import pallas as pl
from jax.experimental.pallas import tpu as pltpu

F32 = jnp.float32
BF16 = jnp.bfloat16

LANES = 128
HEAD_DIM = 64
FOX_HEADS = 8
NSA_HEADS = 8
NSA_KV_HEADS = 2
NSA_GROUP = NSA_HEADS // NSA_KV_HEADS
ROPE_DIM = HEAD_DIM // 4
ROPE_THETA = 500000.0
CMP_BLOCK = 32
CMP_STRIDE = 16
CMP_HIDDEN = 128
SEL_BLOCK = 64
TOP_N = 8
N_LOCAL = 2
WINDOW = 512
N_GROUPS = 4
EXPERTS_PER_GROUP = 4
N_EXPERTS = N_GROUPS * EXPERTS_PER_GROUP
EPS = 1e-6
NEG = -1e30
FORCE = 1e4
SCALE = HEAD_DIM ** -0.5
PAGE_SIZE = 128

FOX_W = FOX_HEADS * HEAD_DIM
NSA_W = NSA_HEADS * HEAD_DIM
KV_W = NSA_KV_HEADS * HEAD_DIM

C_FQ, C_FK, C_FV, C_NQ = 0, 512, 1024, 1536
C_KC, C_VC, C_KS, C_VS, C_KW, C_VW = 2048, 2176, 2304, 2432, 2560, 2688
C_SMALL = 2816
D_IN_PAD = C_SMALL + LANES
SM_LOGF, SM_GATE, SM_CUM = 0, 8, 32


VMEM_LIMIT = 56 * 1024 * 1024


def _params(grid_rank, vmem=VMEM_LIMIT):
    return pltpu.CompilerParams(dimension_semantics=("arbitrary",) * grid_rank, vmem_limit_bytes=vmem)


def _dot(a, b):
    return jnp.dot(a, b, preferred_element_type=F32)


def _dot_t(a, b):
    return lax.dot_general(a, b, (((1,), (1,)), ((), ())), preferred_element_type=F32)


def _split3(x):
    hi = x.astype(BF16)
    r = x - hi.astype(F32)
    mid = r.astype(BF16)
    lo = (r - mid.astype(F32)).astype(BF16)
    return hi, mid, lo


def _dot3(a_f32, b_bf16):
    hi, mid, lo = _split3(a_f32)
    return _dot(hi, b_bf16) + _dot(mid, b_bf16) + _dot(lo, b_bf16)


def _lane(shape):
    return lax.broadcasted_iota(jnp.int32, shape, len(shape) - 1)


def _head_rms(seg, bd, gain):
    ms = _dot((seg * seg).astype(BF16), bd) * (1.0 / HEAD_DIM)
    return seg * lax.rsqrt(ms + EPS) * gain


def _rope(x, cos, sin_m, sin_p):
    return x * cos + pltpu.roll(x, LANES - ROPE_DIM // 2, 1) * sin_m + pltpu.roll(x, ROPE_DIM // 2, 1) * sin_p


def _in_proj_common(x_ref, gat_ref, w_ref, bsm_ref, gq_fox_ref, gk_fox_ref, gq_nsa_ref, gk_sel_ref, gk_win_ref,
                    bd_ref, cos_ref, sinm_ref, sinp_ref):
    x = x_ref[...]
    xn = (x * lax.rsqrt(jnp.mean(x * x, axis=-1, keepdims=True) + EPS) * gat_ref[...]).astype(BF16)
    bd = bd_ref[...]
    bd128 = bd[:LANES, :LANES]
    cos, sin_m, sin_p = cos_ref[...], sinm_ref[...], sinp_ref[...]

    def proj(c0, width):
        return _dot(xn, w_ref[:, c0:c0 + width])

    def normed(c0, gain_ref, chunks):
        outs = []
        for j in range(chunks):
            seg = proj(c0 + 256 * j, 256)
            outs.append(_head_rms(seg, bd, gain_ref[...]))
        return outs

    out = {}
    out["fq"] = normed(C_FQ, gq_fox_ref, 2)
    out["fk"] = normed(C_FK, gk_fox_ref, 2)
    out["fv"] = [proj(C_FV, 256), proj(C_FV + 256, 256)]
    out["nq"] = normed(C_NQ, gq_nsa_ref, 2)
    out["nqr"] = [jnp.concatenate([_rope(c[:, :LANES], cos, sin_m, sin_p), _rope(c[:, LANES:], cos, sin_m, sin_p)], axis=1)
                  for c in out["nq"]]
    out["ckv"] = proj(C_KC, 256)
    ks = proj(C_KS, LANES)
    out["ks"] = _rope(ks * lax.rsqrt(_dot((ks * ks).astype(BF16), bd128) * (1.0 / HEAD_DIM) + EPS) * gk_sel_ref[...],
                      cos, sin_m, sin_p)
    out["vs"] = proj(C_VS, LANES)
    kw = proj(C_KW, LANES)
    out["kw"] = _rope(kw * lax.rsqrt(_dot((kw * kw).astype(BF16), bd128) * (1.0 / HEAD_DIM) + EPS) * gk_win_ref[...],
                      cos, sin_m, sin_p)
    out["vw"] = proj(C_VW, LANES)
    t = proj(C_SMALL, LANES) + bsm_ref[...]
    logf = jnp.minimum(t, 0.0) - jnp.log1p(jnp.exp(-jnp.abs(t)))
    gate = 1.0 / (1.0 + jnp.exp(-t))
    lane = _lane(t.shape)
    out["small"] = jnp.where(lane < SM_GATE, logf, jnp.where(lane < SM_CUM, gate, 0.0))
    return out


def _in_proj_prompt_kernel(x_ref, gat_ref, w_ref, bsm_ref, gq_fox_ref, gk_fox_ref, gq_nsa_ref, gk_sel_ref, gk_win_ref,
                           bd_ref, cos_ref, sinm_ref, sinp_ref, tri_ref,
                           fkv_ref, ckv_ref, skv_ref, wkv_ref, small_ref,
                           fq_ref, fk_ref, fv_ref, nq_ref, ksel_ref, vsel_ref, kwin_ref, vwin_ref,
                           carry_ref, *, tiles_per_seq):
    p = _in_proj_common(x_ref, gat_ref, w_ref, bsm_ref, gq_fox_ref, gk_fox_ref, gq_nsa_ref, gk_sel_ref, gk_win_ref,
                        bd_ref, cos_ref, sinm_ref, sinp_ref)
    tm = x_ref.shape[0]
    lane = _lane((tm, LANES))
    lo_half = lane < HEAD_DIM

    @pl.when(pl.program_id(0) % tiles_per_seq == 0)
    def _():
        carry_ref[...] = jnp.zeros_like(carry_ref)

    small = p["small"]
    hi, mid, lo = _split3(small)
    tri = tri_ref[...]
    cum = _dot(tri, hi) + _dot(tri, mid) + _dot(tri, lo) + carry_ref[...]
    carry_ref[...] = cum[tm - 1:tm, :]
    cum_sh = pltpu.roll(cum, SM_CUM - SM_LOGF, 1)
    small_ref[...] = jnp.where((lane >= SM_CUM) & (lane < SM_CUM + FOX_HEADS), cum_sh, small)

    fkv_ref[:, 0:256] = p["fk"][0]
    fkv_ref[:, 256:512] = p["fk"][1]
    fkv_ref[:, 512:768] = p["fv"][0]
    fkv_ref[:, 768:1024] = p["fv"][1]
    ckv_ref[...] = p["ckv"]
    skv_ref[:, 0:LANES] = p["ks"]
    skv_ref[:, LANES:2 * LANES] = p["vs"]
    wkv_ref[:, 0:LANES] = p["kw"]
    wkv_ref[:, LANES:2 * LANES] = p["vw"]

    def pair(chunks, j):
        c = chunks[j // 2]
        return c[:, (j % 2) * LANES:(j % 2 + 1) * LANES]

    one = jnp.ones((tm, LANES), F32)
    zero = jnp.zeros((tm, LANES), F32)
    ones_col = jnp.where(lane == HEAD_DIM, one, zero)
    for h in range(FOX_HEADS):
        qc, kc, vc = pair(p["fq"], h // 2), pair(p["fk"], h // 2), pair(p["fv"], h // 2)
        if h % 2:
            qc, kc, vc = pltpu.roll(qc, HEAD_DIM, 1), pltpu.roll(kc, HEAD_DIM, 1), pltpu.roll(vc, HEAD_DIM, 1)
        ch = jnp.broadcast_to(cum[:, SM_LOGF + h:SM_LOGF + h + 1], (tm, LANES))
        c_hi, c_mid, c_lo = _split3(ch)
        c_hi, c_mid, c_lo = c_hi.astype(F32), c_mid.astype(F32), c_lo.astype(F32)
        q_ext = jnp.where(lane == 64, c_hi, jnp.where(lane == 65, c_mid, jnp.where(lane == 66, c_lo,
                          jnp.where(lane < 70, one, zero))))
        k_ext = jnp.where(lane < 67, one, jnp.where(lane == 67, -c_hi, jnp.where(lane == 68, -c_mid,
                          jnp.where(lane == 69, -c_lo, zero))))
        fq_ref[0, h] = jnp.where(lo_half, qc * SCALE, q_ext).astype(BF16)
        fk_ref[0, h] = jnp.where(lo_half, kc, k_ext).astype(BF16)
        fv_ref[0, h] = jnp.where(lo_half, vc, ones_col).astype(BF16)
    for h in range(NSA_HEADS):
        a, r = pair(p["nq"], h // 2), pair(p["nqr"], h // 2)
        if h % 2:
            v = jnp.where(lo_half, pltpu.roll(a, HEAD_DIM, 1), r)
        else:
            v = jnp.where(lo_half, a, pltpu.roll(r, HEAD_DIM, 1))
        nq_ref[0, h] = (v * SCALE).astype(BF16)
    for name_k, name_v, k_ref, v_ref in (("ks", "vs", ksel_ref, vsel_ref), ("kw", "vw", kwin_ref, vwin_ref)):
        kk, vv = p[name_k], p[name_v]
        k_ref[0, 0] = jnp.where(lo_half, zero, pltpu.roll(kk, HEAD_DIM, 1)).astype(BF16)
        k_ref[0, 1] = jnp.where(lo_half, zero, kk).astype(BF16)
        v_ref[0, 0] = jnp.where(lo_half, vv, ones_col).astype(BF16)
        v_ref[0, 1] = jnp.where(lo_half, pltpu.roll(vv, HEAD_DIM, 1), ones_col).astype(BF16)


def _in_proj_sample_kernel(x_ref, gat_ref, w_ref, bsm_ref, gq_fox_ref, gk_fox_ref, gq_nsa_ref, gk_sel_ref, gk_win_ref,
                           bd_ref, cos_ref, sinm_ref, sinp_ref,
                           fkv_ref, skv_ref, wkv_ref, small_ref, fq_ref, nqh_ref, qmat_ref,
                           fkv_t_ref, ckv_t_ref, skv_t_ref, wkv_t_ref, small_t_ref):
    p = _in_proj_common(x_ref, gat_ref, w_ref, bsm_ref, gq_fox_ref, gk_fox_ref, gq_nsa_ref, gk_sel_ref, gk_win_ref,
                        bd_ref, cos_ref, sinm_ref, sinp_ref)
    small_ref[...] = p["small"]
    small_t_ref[...] = p["small"].T
    for j in range(2):
        fkv_ref[:, 256 * j:256 * (j + 1)] = p["fk"][j]
        fkv_ref[:, 512 + 256 * j:512 + 256 * (j + 1)] = p["fv"][j]
        fkv_t_ref[256 * j:256 * (j + 1), :] = p["fk"][j].T
        fkv_t_ref[512 + 256 * j:512 + 256 * (j + 1), :] = p["fv"][j].T
        fq_ref[:, 256 * j:256 * (j + 1)] = p["fq"][j] * SCALE
    ckv_t_ref[...] = p["ckv"].T
    for ref, ref_t, k_name, v_name in ((skv_ref, skv_t_ref, "ks", "vs"), (wkv_ref, wkv_t_ref, "kw", "vw")):
        ref[:, 0:LANES] = p[k_name]
        ref[:, LANES:2 * LANES] = p[v_name]
        ref_t[0:LANES, :] = p[k_name].T
        ref_t[LANES:2 * LANES, :] = p[v_name].T
    tm = x_ref.shape[0]
    lo_half = _lane((tm, LANES)) < HEAD_DIM
    zero = jnp.zeros((tm, LANES), F32)
    for h in range(NSA_HEADS):
        a = p["nq"][h // 4][:, (h // 2 % 2) * LANES:(h // 2 % 2 + 1) * LANES]
        r = p["nqr"][h // 4][:, (h // 2 % 2) * LANES:(h // 2 % 2 + 1) * LANES]
        if h % 2:
            v = jnp.where(lo_half, pltpu.roll(a, HEAD_DIM, 1), r)
        else:
            v = jnp.where(lo_half, a, pltpu.roll(r, HEAD_DIM, 1))
        nqh_ref[h] = (v * SCALE).astype(BF16)
        kvh = h // NSA_GROUP
        rr = r if (h % 2) == kvh else pltpu.roll(r, HEAD_DIM, 1)
        rr = jnp.where(lo_half, rr, zero) if kvh == 0 else jnp.where(lo_half, zero, rr)
        qmat_ref[h, :, 0:LANES] = (rr * SCALE).astype(BF16)
        qmat_ref[h, :, LANES:2 * LANES] = zero.astype(BF16)


def _const_spec(shape):
    nd = len(shape)
    return pl.BlockSpec(shape, lambda *_: (0,) * nd)


def _rope_tables(pos):
    half = ROPE_DIM // 2
    inv = ROPE_THETA ** (-jnp.arange(half, dtype=F32) * 2.0 / ROPE_DIM)
    ang = pos.astype(F32)[:, None] * inv[None, :]
    cos, sin = jnp.cos(ang), jnp.sin(ang)
    n = pos.shape[0]
    rest = HEAD_DIM - ROPE_DIM
    cos_h = jnp.concatenate([cos, cos, jnp.ones((n, rest), F32)], axis=1)
    sinm_h = jnp.concatenate([-sin, jnp.zeros((n, half + rest), F32)], axis=1)
    sinp_h = jnp.concatenate([jnp.zeros((n, half), F32), sin, jnp.zeros((n, rest), F32)], axis=1)
    return tuple(jnp.concatenate([t, t], axis=1) for t in (cos_h, sinm_h, sinp_h))


def _in_proj_operands(g_attn, w_in, b_forget, b_gate, g_q_fox, g_k_fox, g_q_nsa, g_k_sel, g_k_win):
    d_model = w_in.shape[0]
    sizes = (FOX_W, FOX_W, FOX_W, FOX_HEADS, NSA_W, KV_W, KV_W, KV_W, KV_W, KV_W, KV_W, 3 * NSA_HEADS)
    offs = np.concatenate([[0], np.cumsum(sizes)])
    fq, fk, fv, ff, nq, kc, vc, ks, vs, kw, vw, ng = [w_in[:, offs[i]:offs[i + 1]] for i in range(len(sizes))]
    pad = jnp.zeros((d_model, LANES - FOX_HEADS - 3 * NSA_HEADS), w_in.dtype)
    w = jnp.concatenate([fq, fk, fv, nq, kc, vc, ks, vs, kw, vw, ff, ng, pad], axis=1).astype(BF16)
    bsm = jnp.concatenate([b_forget, b_gate, jnp.zeros((LANES - FOX_HEADS - 3 * NSA_HEADS,), F32)]).reshape(1, LANES)
    hd = np.arange(256) // HEAD_DIM
    bd = jnp.asarray(hd[:, None] == hd[None, :], BF16)
    t256 = lambda g: jnp.tile(g, 256 // HEAD_DIM).reshape(1, 256)
    t128 = lambda g: jnp.tile(g, LANES // HEAD_DIM).reshape(1, LANES)
    return (g_attn.reshape(1, d_model), w, bsm, t256(g_q_fox), t256(g_k_fox), t256(g_q_nsa), t128(g_k_sel), t128(g_k_win), bd)


def _in_proj_prompt(x, ops, tm):
    b, s, d = x.shape
    n = b * s
    tps = s // tm
    cos, sinm, sinp = _rope_tables(jnp.arange(s, dtype=jnp.int32))
    tri = jnp.asarray(np.tril(np.ones((tm, tm), np.float32)), BF16)
    tab_spec = pl.BlockSpec((tm, LANES), lambda i: (i % tps, 0))
    in_specs = ([pl.BlockSpec((tm, d), lambda i: (i, 0))] + [_const_spec(o.shape) for o in ops]
                + [tab_spec, tab_spec, tab_spec, _const_spec((tm, tm))])
    row = lambda w: pl.BlockSpec((tm, w), lambda i: (i, 0))
    head = lambda nh: pl.BlockSpec((1, nh, tm, LANES), lambda i: (i // tps, 0, i % tps, 0))
    hshape = lambda nh: jax.ShapeDtypeStruct((b, nh, s, LANES), BF16)
    out_specs = [row(2 * FOX_W), row(2 * KV_W), row(2 * KV_W), row(2 * KV_W), row(LANES),
                 head(FOX_HEADS), head(FOX_HEADS), head(FOX_HEADS), head(NSA_HEADS),
                 head(NSA_KV_HEADS), head(NSA_KV_HEADS), head(NSA_KV_HEADS), head(NSA_KV_HEADS)]
    out_shape = [jax.ShapeDtypeStruct((n, 2 * FOX_W), F32)] + [jax.ShapeDtypeStruct((n, 2 * KV_W), F32)] * 3 + [
        jax.ShapeDtypeStruct((n, LANES), F32), hshape(FOX_HEADS), hshape(FOX_HEADS), hshape(FOX_HEADS), hshape(NSA_HEADS),
        hshape(NSA_KV_HEADS), hshape(NSA_KV_HEADS), hshape(NSA_KV_HEADS), hshape(NSA_KV_HEADS)]
    return pl.pallas_call(
        functools.partial(_in_proj_prompt_kernel, tiles_per_seq=tps),
        grid=(n // tm,), in_specs=in_specs, out_specs=out_specs, out_shape=out_shape,
        scratch_shapes=[pltpu.VMEM((1, LANES), F32)],
        compiler_params=_params(1),
        name="in_proj_prompt",
    )(x.reshape(n, d), *ops, cos, sinm, sinp, tri)


def _in_proj_sample(x, ops, past):
    b, q, d = x.shape
    n = b * q
    pos = past + jnp.arange(q, dtype=jnp.int32)
    cos, sinm, sinp = (jnp.tile(t, (b, 1)) for t in _rope_tables(pos))
    full = lambda shape: pl.BlockSpec(shape, lambda i: (0, 0))
    in_specs = [full((n, d))] + [_const_spec(o.shape) for o in ops] + [full((n, LANES))] * 3
    widths = (2 * FOX_W, 2 * KV_W, 2 * KV_W, LANES, FOX_W)
    widths_t = (2 * FOX_W, 2 * KV_W, 2 * KV_W, 2 * KV_W, LANES)
    hm = lambda w: pl.BlockSpec((NSA_HEADS, n, w), lambda i: (0, 0, 0))
    return pl.pallas_call(
        _in_proj_sample_kernel, grid=(1,), in_specs=in_specs,
        out_specs=[full((n, w)) for w in widths] + [hm(LANES), hm(2 * LANES)] + [full((w, n)) for w in widths_t],
        out_shape=[jax.ShapeDtypeStruct((n, w), F32) for w in widths] + [
            jax.ShapeDtypeStruct((NSA_HEADS, n, LANES), BF16), jax.ShapeDtypeStruct((NSA_HEADS, n, 2 * LANES), BF16)] + [
            jax.ShapeDtypeStruct((w, n), F32) for w in widths_t],
        compiler_params=_params(1),
        name="in_proj_sample",
    )(x.reshape(n, d), *ops, cos, sinm, sinp)


def _flash_step(q, k, v, m, acc, mask=None):
    s = _dot_t(q, k)
    if mask is not None:
        s = jnp.where(mask, s, NEG)
    m_new = jnp.maximum(m, jnp.max(s, axis=-1, keepdims=True))
    p = jnp.exp(s - m_new)
    acc = acc * jnp.exp(m - m_new) + _dot(p.astype(BF16), v)
    return m_new, acc


def _finish(acc):
    return acc / acc[:, HEAD_DIM:HEAD_DIM + 1]


def _pack_heads(a, b):
    return jnp.where(_lane(a.shape) < HEAD_DIM, a, pltpu.roll(b, HEAD_DIM, 1))


def _fox_prompt_kernel(q_ref, k_ref, v_ref, o_ref, *, tile):
    qi = pl.program_id(2)
    rowi = lax.broadcasted_iota(jnp.int32, (tile, tile), 0)
    coli = lax.broadcasted_iota(jnp.int32, (tile, tile), 1)
    outs = []
    for j in range(2):
        q = q_ref[0, j]

        def body(t, carry, j=j, q=q):
            off = pl.multiple_of(t * tile, tile)
            return _flash_step(q, k_ref[0, j, pl.ds(off, tile), :], v_ref[0, j, pl.ds(off, tile), :], *carry)

        init = (jnp.full((tile, 1), NEG, F32), jnp.zeros((tile, LANES), F32))
        m, acc = lax.fori_loop(0, qi, body, init)
        off = pl.multiple_of(qi * tile, tile)
        m, acc = _flash_step(q, k_ref[0, j, pl.ds(off, tile), :], v_ref[0, j, pl.ds(off, tile), :], m, acc,
                             mask=coli <= rowi)
        outs.append(_finish(acc))
    o_ref[0] = _pack_heads(outs[0], outs[1])


def _fox_prompt(fq, fk, fv, tile):
    b, h, s, _ = fq.shape
    qspec = pl.BlockSpec((1, 2, tile, LANES), lambda bi, hi, qi: (bi, hi, qi, 0))
    kspec = pl.BlockSpec((1, 2, s, LANES), lambda bi, hi, qi: (bi, hi, 0, 0))
    return pl.pallas_call(
        functools.partial(_fox_prompt_kernel, tile=tile),
        grid=(b, h // 2, s // tile), in_specs=[qspec, kspec, kspec],
        out_specs=pl.BlockSpec((1, tile, LANES), lambda bi, hi, qi: (bi, qi, hi)),
        out_shape=jax.ShapeDtypeStruct((b, s, h * HEAD_DIM), F32),
        compiler_params=_params(3),
        name="fox_prompt",
    )(fq, fk, fv)


def _compress_operands(pos_k, w_k1, w_k2, pos_v, w_v1, w_v2, g_k_cmp):
    half = CMP_BLOCK // 2
    wk = jnp.stack([w_k1, w_k1, w_v1, w_v1]).reshape(4, 2, half, HEAD_DIM, CMP_HIDDEN)
    wbig = jnp.einsum("kardh,kj->rkdajh", wk, jnp.eye(4, dtype=w_k1.dtype))
    wbig = wbig.reshape(half * 4 * HEAD_DIM, 2 * 4 * CMP_HIDDEN).astype(BF16)
    pos = jnp.stack([pos_k, pos_k, pos_v, pos_v]).reshape(4, 2, half, HEAD_DIM)
    pos_rows = jnp.transpose(pos, (1, 2, 0, 3)).reshape(2, half * 4 * HEAD_DIM)
    pos_rows = jnp.concatenate([pos_rows, jnp.zeros((6, pos_rows.shape[1]), pos_rows.dtype)]).astype(BF16)
    pad2 = lambda w: jnp.pad(w, ((0, 0), (0, LANES - HEAD_DIM))).astype(BF16)
    gain = jnp.pad(g_k_cmp, (0, LANES - HEAD_DIM)).reshape(1, LANES)
    return wbig, pos_rows, pad2(w_k2), pad2(w_v2), gain


def _compress_core(x, w_ref, pos_ref, w2k_ref, w2v_ref, gain_ref, kc_ref, vc_ref):
    r = x.shape[0]
    hw = 4 * CMP_HIDDEN
    ab = _dot(x.astype(BF16), w_ref[...])
    pw = _dot(pos_ref[...], w_ref[...])
    hid = ab[:, :hw] + pltpu.roll(ab[:, hw:], r - 1, 0) + pw[0:1, :hw] + pw[1:2, hw:]
    hid = (hid * (1.0 / (1.0 + jnp.exp(-hid)))).astype(BF16)
    lane = _lane((r, LANES))
    for kvh in range(NSA_KV_HEADS):
        kk = _dot(hid[:, kvh * CMP_HIDDEN:(kvh + 1) * CMP_HIDDEN], w2k_ref[...])
        ms = jnp.sum(kk * kk, axis=-1, keepdims=True) * (1.0 / HEAD_DIM)
        kc_ref[0, kvh] = (kk * lax.rsqrt(ms + EPS) * gain_ref[...]).astype(BF16)
        vv = _dot(hid[:, (2 + kvh) * CMP_HIDDEN:(3 + kvh) * CMP_HIDDEN], w2v_ref[...])
        vc_ref[0, kvh] = jnp.where(lane == HEAD_DIM, 1.0, vv).astype(BF16)


def _compress_prompt_kernel(x_ref, w_ref, pos_ref, w2k_ref, w2v_ref, gain_ref, kc_ref, vc_ref):
    _compress_core(x_ref[0], w_ref, pos_ref, w2k_ref, w2v_ref, gain_ref, kc_ref, vc_ref)


def _compress_prompt(ckv, cops, b):
    n = ckv.shape[0]
    r = n // b // CMP_STRIDE
    width = CMP_STRIDE * ckv.shape[1]
    out = jax.ShapeDtypeStruct((b, NSA_KV_HEADS, r, LANES), BF16)
    ospec = pl.BlockSpec((1, NSA_KV_HEADS, r, LANES), lambda i: (i, 0, 0, 0))
    return pl.pallas_call(
        _compress_prompt_kernel, grid=(b,),
        in_specs=[pl.BlockSpec((1, r, width), lambda i: (i, 0, 0))] + [_const_spec(o.shape) for o in cops],
        out_specs=[ospec, ospec], out_shape=[out, out],
        compiler_params=_params(1),
        name="compress_prompt",
    )(ckv.reshape(b, r, width), *cops)


def _cmp_branch(q, kc, vc, qpos_rows, n_valid_limit):
    s = _dot_t(q, kc)
    cidx = _lane(s.shape)
    valid = (cidx * CMP_STRIDE + (CMP_BLOCK - 1) <= qpos_rows) & (cidx < n_valid_limit)
    s = jnp.where(valid, s, NEG)
    p = jnp.exp(s - jnp.max(s, axis=-1, keepdims=True)) * valid.astype(F32)
    l = jnp.sum(p, axis=-1, keepdims=True)
    p = p / jnp.where(l > 0, l, 1.0)
    return _dot(p.astype(BF16), vc), p


def _select_blocks(imp, qblk, n_sel):
    j = _lane(imp.shape)
    dist = qblk - j
    forced = (j == 0) | ((dist >= 0) & (dist < N_LOCAL))
    score = jnp.where(dist >= 0, jnp.where(forced, FORCE, imp), NEG)
    score = jnp.where(j < n_sel, score, -jnp.inf)
    cnt = jnp.zeros(imp.shape, F32)
    for i in range(n_sel):
        col = score[:, i:i + 1]
        beats = (col > score) | ((col == score) & (j > i))
        cnt = cnt + beats.astype(F32)
    return (cnt < min(TOP_N, n_sel)) & (j < n_sel)


def _select_blocks_t(imp_t, qblk, n_sel):
    j = lax.broadcasted_iota(jnp.int32, imp_t.shape, 0)
    dist = qblk - j
    forced = (j == 0) | ((dist >= 0) & (dist < N_LOCAL))
    score = jnp.where(dist >= 0, jnp.where(forced, FORCE, imp_t), NEG)
    cnt = jnp.zeros(imp_t.shape, F32)
    for i in range(n_sel):
        row = jnp.broadcast_to(score[i:i + 1, :], score.shape)
        beats = (row > score) | ((row == score) & (j > i))
        cnt = cnt + beats.astype(F32)
    return cnt < min(TOP_N, n_sel)


def _nsa_prompt_kernel(q_ref, kc_ref, vc_ref, ks_ref, vs_ref, kw_ref, vw_ref, sm_ref, msel_ref, exp_ref, o_ref, *,
                       tq, tk, n_cmp, n_sel):
    i = pl.program_id(1)
    start = i * tq
    g = NSA_GROUP
    rows = g * tq
    s_len = ks_ref.shape[2]
    qpos = start + lax.broadcasted_iota(jnp.int32, (tq, 1), 0)
    qpos_rows = jnp.concatenate([qpos] * g, axis=0)
    qblk_t = (start + lax.broadcasted_iota(jnp.int32, (1, tq), 1)) // SEL_BLOCK
    small = sm_ref[...]
    kpos_l = lax.broadcasted_iota(jnp.int32, (tq, tk), 1)
    band = min(WINDOW + tq, s_len)
    kpos_w = lax.broadcasted_iota(jnp.int32, (tq, band), 1)
    init = (jnp.full((rows, 1), NEG, F32), jnp.zeros((rows, LANES), F32))
    heads = []
    for k in range(NSA_KV_HEADS):
        q = q_ref[0, k * g:(k + 1) * g].reshape(rows, LANES)
        o_c, p = _cmp_branch(q, kc_ref[0, k], vc_ref[0, k], qpos_rows, n_cmp)
        psum = p[0:tq]
        for gi in range(1, g):
            psum = psum + p[gi * tq:(gi + 1) * tq]
        hi = psum.astype(BF16)
        lo = (psum - hi.astype(F32)).astype(BF16)
        imp = _dot(hi, msel_ref[...]) + _dot(lo, msel_ref[...])
        sel_t = _select_blocks_t(imp.T[0:n_sel], qblk_t, n_sel)
        sel_t = jnp.concatenate([sel_t.astype(F32), jnp.zeros((LANES - n_sel, tq), F32)], axis=0)
        selb = sel_t.T.astype(BF16)

        def sel_body(t, carry, k=k, q=q, selb=selb):
            off = pl.multiple_of(t * tk, tk)
            picked = _dot(selb, exp_ref[t]) > 0.5
            ok = picked & (t * tk + kpos_l <= qpos)
            s = _dot_t(q, ks_ref[0, k, pl.ds(off, tk), :]).reshape(g, tq, tk)
            s = jnp.where(ok[None], s, NEG).reshape(rows, tk)
            m, acc = carry
            m_new = jnp.maximum(m, jnp.max(s, axis=-1, keepdims=True))
            pr = jnp.exp(s - m_new)
            acc = acc * jnp.exp(m - m_new) + _dot(pr.astype(BF16), vs_ref[0, k, pl.ds(off, tk), :])
            return m_new, acc

        _, acc_s = lax.fori_loop(0, (start + tq + tk - 1) // tk, sel_body, init)
        off = pl.multiple_of(jnp.maximum(start + tq - band, 0), tq)
        diff = qpos - (off + kpos_w)
        ok = (diff >= 0) & (diff < WINDOW)
        s = _dot_t(q, kw_ref[0, k, pl.ds(off, band), :]).reshape(g, tq, band)
        s = jnp.where(ok[None], s, NEG).reshape(rows, band)
        pr = jnp.exp(s - jnp.max(s, axis=-1, keepdims=True))
        acc_w = _dot(pr.astype(BF16), vw_ref[0, k, pl.ds(off, band), :])
        o_s, o_w = _finish(acc_s), _finish(acc_w)
        for gi in range(g):
            h = k * g + gi
            sl = slice(gi * tq, (gi + 1) * tq)
            c0 = SM_GATE + 3 * h
            heads.append(small[:, c0:c0 + 1] * o_c[sl] + small[:, c0 + 1:c0 + 2] * o_s[sl] + small[:, c0 + 2:c0 + 3] * o_w[sl])
    for c in range(NSA_HEADS // 2):
        o_ref[0, :, c * LANES:(c + 1) * LANES] = _pack_heads(heads[2 * c], heads[2 * c + 1])


def _cmp_to_sel_matrix(n_rows, n_sel):
    lead = CMP_BLOCK // CMP_STRIDE - 1
    r = SEL_BLOCK // CMP_STRIDE
    c = np.arange(n_rows)[:, None]
    j = np.arange(n_sel)[None, :]
    return jnp.asarray((c >= r * j - lead) & (c <= r * j + r - 1), BF16)


def _nsa_prompt(nq, kc, vc, ksel, vsel, kwin, vwin, small, tq):
    b, h, s, _ = nq.shape
    nb = s // tq
    r = kc.shape[2]
    n_cmp = (s - CMP_BLOCK) // CMP_STRIDE + 1
    n_sel = s // SEL_BLOCK
    assert n_sel <= LANES
    tk = _row_tile(s, 512)
    msel = _cmp_to_sel_matrix(r, LANES)
    key_blk = (np.arange(s) // SEL_BLOCK).reshape(s // tk, 1, tk)
    expand = jnp.asarray(np.arange(LANES).reshape(1, LANES, 1) == key_blk, BF16)
    seq_spec = lambda rows: pl.BlockSpec((1, NSA_KV_HEADS, rows, LANES), lambda bi, qi: (bi, 0, 0, 0))
    return pl.pallas_call(
        functools.partial(_nsa_prompt_kernel, tq=tq, tk=tk, n_cmp=n_cmp, n_sel=n_sel),
        grid=(b, nb),
        in_specs=[pl.BlockSpec((1, h, tq, LANES), lambda bi, qi: (bi, 0, qi, 0)), seq_spec(r), seq_spec(r),
                  seq_spec(s), seq_spec(s), seq_spec(s), seq_spec(s),
                  pl.BlockSpec((tq, LANES), lambda bi, qi: (bi * nb + qi, 0)), _const_spec(msel.shape),
                  _const_spec(expand.shape)],
        out_specs=pl.BlockSpec((1, tq, h * HEAD_DIM), lambda bi, qi: (bi, qi, 0)),
        out_shape=jax.ShapeDtypeStruct((b, s, h * HEAD_DIM), F32),
        compiler_params=_params(2),
        name="nsa_prompt",
    )(nq, kc, vc, ksel, vsel, kwin, vwin, small, msel, expand)


def _rms(x, gain):
    return x * lax.rsqrt(jnp.mean(x * x, axis=-1, keepdims=True) + EPS) * gain


def _first_lane_where(cond, lane):
    return jnp.min(jnp.where(cond, lane, LANES), axis=-1, keepdims=True)


def _post_attn_kernel(of_ref, on_ref, x_ref, gf_ref, gn_ref, wo_ref, gffn_ref, wr_hi_ref, wr_lo_ref, br_ref,
                      x1_ref, h_ref, gate_ref):
    nf = _rms(of_ref[...], gf_ref[...]).astype(BF16)
    nn = _rms(on_ref[...], gn_ref[...]).astype(BF16)
    x1 = x_ref[...] + _dot(nf, wo_ref[0:FOX_W, :]) + _dot(nn, wo_ref[FOX_W:FOX_W + NSA_W, :])
    x1_ref[...] = x1
    h = _rms(x1, gffn_ref[...])
    h_ref[...] = h.astype(BF16)
    h_hi = h.astype(BF16)
    h_lo = (h - h_hi.astype(F32)).astype(BF16)
    logits = _dot(h_hi, wr_hi_ref[...]) + _dot(h_lo, wr_hi_ref[...]) + _dot(h_hi, wr_lo_ref[...]) + br_ref[...]
    lane = _lane(logits.shape)
    is_grp = (lane >= N_EXPERTS) & (lane < N_EXPERTS + N_GROUPS)
    lg = jnp.where(is_grp, logits, NEG)
    lg_max = jnp.max(lg, axis=-1, keepdims=True)
    gsel = _first_lane_where(is_grp & (lg == lg_max), lane) - N_EXPERTS
    pg_sel = 1.0 / jnp.sum(jnp.where(is_grp, jnp.exp(lg - lg_max), 0.0), axis=-1, keepdims=True)
    in_grp = (lane >= gsel * EXPERTS_PER_GROUP) & (lane < (gsel + 1) * EXPERTS_PER_GROUP)
    le = jnp.where(in_grp, logits, NEG)
    pe = jnp.where(in_grp, jnp.exp(le - jnp.max(le, axis=-1, keepdims=True)), 0.0)
    pe = pe / jnp.sum(pe, axis=-1, keepdims=True)
    pe = jnp.where(in_grp, pe, -1.0)
    v1 = jnp.max(pe, axis=-1, keepdims=True)
    i1 = _first_lane_where(pe == v1, lane)
    rest = jnp.where(lane == i1, -1.0, pe)
    v2 = jnp.max(rest, axis=-1, keepdims=True)
    i2 = _first_lane_where(rest == v2, lane)
    tot = v1 + v2
    gate_ref[...] = jnp.where(lane == i1, v1 / tot * pg_sel, jnp.where(lane == i2, v2 / tot * pg_sel, 0.0))


def _moe_kernel(h_ref, gate_ref, x1_ref, wgu_ref, wd_ref, y_ref):
    e = pl.program_id(1)

    @pl.when(e == 0)
    def _():
        y_ref[...] = x1_ref[...]

    d_exp = wd_ref.shape[1]
    gu = _dot(h_ref[...], wgu_ref[0])
    gpre, up = gu[:, :d_exp], gu[:, d_exp:]
    gate = gate_ref[...]
    ge = jnp.sum(jnp.where(_lane(gate.shape) == e, gate, 0.0), axis=-1, keepdims=True)
    hid = gpre * (1.0 / (1.0 + jnp.exp(-gpre))) * up * ge
    y_ref[...] += _dot(hid.astype(BF16), wd_ref[0])


def _finish_operands(g_out_fox, g_out_nsa, w_out, g_ffn, w_rg, b_rg, w_re, b_re, w_gate, w_up, w_down):
    d = w_out.shape[1]
    pad = LANES - N_EXPERTS - N_GROUPS
    wr = jnp.concatenate([w_re, w_rg, jnp.zeros((d, pad), F32)], axis=1)
    wr_hi = wr.astype(BF16)
    wr_lo = (wr - wr_hi.astype(F32)).astype(BF16)
    br = jnp.concatenate([b_re, b_rg, jnp.zeros((pad,), F32)]).reshape(1, LANES)
    post = (g_out_fox.reshape(1, -1), g_out_nsa.reshape(1, -1), w_out.astype(BF16), g_ffn.reshape(1, d), wr_hi, wr_lo, br)
    moe = (jnp.concatenate([w_gate, w_up], axis=-1).astype(BF16), w_down.astype(BF16))
    return post, moe


def _finish_layer(o_fox, o_nsa, x, post_ops, moe_ops, tm):
    n, d = x.shape
    tp = _row_tile(n, 512)
    row = lambda w: pl.BlockSpec((tp, w), lambda i: (i, 0))
    x1, h, gate = pl.pallas_call(
        _post_attn_kernel, grid=(n // tp,),
        in_specs=[row(FOX_W), row(NSA_W), row(d)] + [_const_spec(o.shape) for o in post_ops],
        out_specs=[row(d), row(d), row(LANES)],
        out_shape=[jax.ShapeDtypeStruct((n, d), F32), jax.ShapeDtypeStruct((n, d), BF16), jax.ShapeDtypeStruct((n, LANES), F32)],
        compiler_params=_params(1),
        name="post_attn",
    )(o_fox, o_nsa, x, *post_ops)
    wgu, wd = moe_ops
    tok = lambda w: pl.BlockSpec((tm, w), lambda i, e: (i, 0))
    return pl.pallas_call(
        _moe_kernel, grid=(n // tm, wgu.shape[0]),
        in_specs=[tok(d), tok(LANES), tok(d),
                  pl.BlockSpec((1,) + wgu.shape[1:], lambda i, e: (e, 0, 0)),
                  pl.BlockSpec((1,) + wd.shape[1:], lambda i, e: (e, 0, 0))],
        out_specs=tok(d), out_shape=jax.ShapeDtypeStruct((n, d), F32),
        compiler_params=_params(2),
        name="moe",
    )(h, gate, x1, wgu, wd)


def _row_tile(n, cap):
    for t in range(min(cap, n), 7, -1):
        if n % t == 0 and t % 8 == 0:
            return t
    return n


def _logf_suffix_kernel(lf_ref, u_ref, ones_ref, w_ref, tot_ref):
    hi, mid, lo = _split3(lf_ref[...])
    u, o = u_ref[...], ones_ref[...]
    w_ref[...] = _dot(hi, u) + _dot(mid, u) + _dot(lo, u)
    tot_ref[...] = _dot(hi, o) + _dot(mid, o) + _dot(lo, o)


def _logf_suffix(logf_t):
    n_phys, nh, page = logf_t.shape
    rows = n_phys * nh
    t = np.arange(page)
    u = jnp.asarray(t[:, None] > t[None, :], BF16)
    ones = jnp.ones((page, page), BF16)
    tr = _row_tile(rows, 2048)
    spec = pl.BlockSpec((tr, page), lambda i: (i, 0))
    out = jax.ShapeDtypeStruct((rows, page), F32)
    w, tot = pl.pallas_call(
        _logf_suffix_kernel, grid=(rows // tr,),
        in_specs=[spec, _const_spec(u.shape), _const_spec(ones.shape)],
        out_specs=[spec, spec], out_shape=[out, out],
        compiler_params=_params(1),
        name="logf_suffix",
    )(logf_t.reshape(rows, page), u, ones)
    return w.reshape(n_phys, nh, page), tot.reshape(n_phys, nh, page)


def _per_head(col):
    nh, w = col.shape
    return jnp.broadcast_to(col[:, None, :], (nh, HEAD_DIM, w)).reshape(nh * HEAD_DIM, w)


def _head_sums(x):
    return jnp.sum(x.reshape(x.shape[0] // HEAD_DIM, HEAD_DIM, x.shape[1]), axis=1)


def _fox_sample_kernel(pt_ref, q_ref, new_ref, lfn_ref, *rest, pc):
    del pt_ref
    kv_refs, w_refs, tot_refs = rest[:pc], rest[pc:2 * pc], rest[2 * pc:3 * pc]
    o_ref = rest[3 * pc]
    qb_ref, m_ref, carry_ref, acc_ref, l_ref = rest[3 * pc + 1:]
    c = pl.program_id(1)
    page = acc_ref.shape[1]

    @pl.when(c == 0)
    def _():
        qb_ref[...] = jnp.broadcast_to(q_ref[0], qb_ref.shape)
        m_ref[...] = jnp.full(m_ref.shape, NEG, F32)
        carry_ref[...] = jnp.zeros_like(carry_ref)
        acc_ref[...] = jnp.zeros_like(acc_ref)
        l_ref[...] = jnp.zeros_like(l_ref)

    qb = qb_ref[...]
    carry = carry_ref[...]
    scores = []
    for j in range(pc):
        scores.append(_head_sums(kv_refs[j][0, 0] * qb) + (lfn_ref[0] + carry + w_refs[j][0]))
        carry = carry + tot_refs[j][0]
    carry_ref[...] = carry
    m_chunk = scores[0]
    for s in scores[1:]:
        m_chunk = jnp.maximum(m_chunk, s)
    m_old = m_ref[...]
    m_new = jnp.maximum(m_old, jnp.max(m_chunk, axis=-1, keepdims=True))
    m_ref[...] = m_new
    alpha = jnp.exp(m_old - m_new)
    acc = acc_ref[...] * _per_head(alpha)
    l = l_ref[...] * alpha
    for j in range(pc):
        p = jnp.exp(scores[j] - m_new)
        l = l + p
        acc = acc + _per_head(p) * kv_refs[j][0, 1]
    acc_ref[...] = acc
    l_ref[...] = l

    @pl.when(c == pl.num_programs(1) - 1)
    def _():
        width = qb.shape[0]
        k_new, v_new = new_ref[0, 0:width, :], new_ref[0, width:2 * width, :]
        s_new = _head_sums(k_new * q_ref[0])
        m_fin = jnp.maximum(m_new, s_new)
        a = jnp.exp(m_new - m_fin)
        p_new = jnp.exp(s_new - m_fin)
        l_tot = jnp.sum(l, axis=-1, keepdims=True) * a + p_new
        acc_col = jnp.sum(acc, axis=-1, keepdims=True) * _per_head(a) + _per_head(p_new) * v_new
        o_ref[0] = acc_col / _per_head(l_tot)


def _fox_sample(fq, fkv_new, lf_new, kv_t, w_suffix, tot, page_table, pc):
    db = fq.shape[0]
    n_phys, _, width, page = kv_t.shape
    nh = w_suffix.shape[1]
    n_pages = page_table.shape[1]
    phys = lambda s, c, pt, j: pt[jnp.minimum(s, db - 1), jnp.clip(n_pages - 1 - (c * pc + j), 0, n_pages - 1)]
    per_seq = lambda r: pl.BlockSpec((1, r, 1), lambda s, c, pt: (s, 0, 0))
    in_specs = [per_seq(width), per_seq(2 * width), per_seq(nh)]
    in_specs += [pl.BlockSpec((1, 2, width, page), lambda s, c, pt, j=j: (phys(s, c, pt, j), 0, 0, 0)) for j in range(pc)]
    in_specs += [pl.BlockSpec((1, nh, page), lambda s, c, pt, j=j: (phys(s, c, pt, j), 0, 0)) for j in list(range(pc)) * 2]
    out = pl.pallas_call(
        functools.partial(_fox_sample_kernel, pc=pc),
        grid_spec=pltpu.PrefetchScalarGridSpec(
            num_scalar_prefetch=1, grid=(db, n_pages // pc), in_specs=in_specs,
            out_specs=pl.BlockSpec((1, width, 1), lambda s, c, pt: (s, 0, 0)),
            scratch_shapes=[pltpu.VMEM((width, page), F32), pltpu.VMEM((nh, 1), F32), pltpu.VMEM((nh, page), F32),
                            pltpu.VMEM((width, page), F32), pltpu.VMEM((nh, page), F32)]),
        out_shape=jax.ShapeDtypeStruct((db, width, 1), F32),
        compiler_params=_params(2),
        name="fox_sample",
    )(page_table, fq.reshape(db, width, 1), fkv_new.reshape(db, 2 * width, 1), lf_new.reshape(db, nh, 1),
      *([kv_t] * pc), *([w_suffix] * pc), *([tot] * pc))
    return out.reshape(db, width)


def _nsa_sample_cmp_kernel(pt_ref, q_ref, wbig_ref, posr_ref, w2k_ref, w2v_ref, gain_ref, msel_ref, utri_ref, *rest,
                           n_pages, past, n_cmp, n_sel):
    del pt_ref
    pages = rest[:n_pages]
    oc_ref, idx_ref = rest[n_pages:n_pages + 2]
    x_ref, kc_ref, vc_ref = rest[n_pages + 2:]
    rpp = pages[0].shape[1]
    for j in range(n_pages):
        x_ref[j * rpp:(j + 1) * rpp, :] = pages[j][0]
    _compress_core(x_ref[...], wbig_ref, posr_ref, w2k_ref, w2v_ref, gain_ref, kc_ref, vc_ref)
    q8 = q_ref[0]
    row = lax.broadcasted_iota(jnp.int32, (NSA_HEADS, 1), 0)
    lane_row = _lane((1, LANES))
    oc = jnp.zeros((NSA_HEADS, LANES), F32)
    idx_row = jnp.zeros((1, LANES), jnp.int32)
    for k in range(NSA_KV_HEADS):
        o_k, p = _cmp_branch(q8, kc_ref[0, k], vc_ref[0, k], past, n_cmp)
        in_grp = row // NSA_GROUP == k
        oc = jnp.where(in_grp, o_k, oc)
        psum = jnp.broadcast_to(jnp.sum(jnp.where(in_grp, p, 0.0), axis=0, keepdims=True), p.shape)
        hi = psum.astype(BF16)
        lo = (psum - hi.astype(F32)).astype(BF16)
        imp = _dot(hi, msel_ref[...]) + _dot(lo, msel_ref[...])
        sel = _select_blocks(imp, past // SEL_BLOCK, n_sel)
        rank = _dot(sel.astype(BF16), utri_ref[...])
        j = _lane(imp.shape)
        for n in range(TOP_N):
            val = jnp.sum(jnp.where(sel & (rank == n), j, 0), axis=-1, keepdims=True)
            idx_row = jnp.where(lane_row == k * TOP_N + n, val[0:1], idx_row)
    oc_ref[0] = oc
    idx_ref[0] = idx_row


def _nsa_sample_cmp(nqh, cache_cmp, page_table, cops, past):
    db = nqh.shape[0]
    n_phys, page = cache_cmp.shape[0], cache_cmp.shape[1]
    n_pages = page_table.shape[1]
    rpp = page // CMP_STRIDE
    width = CMP_STRIDE * 2 * KV_W
    r = n_pages * rpp
    n_cmp = (past + 1 - CMP_BLOCK) // CMP_STRIDE + 1
    n_sel = past // SEL_BLOCK + 1
    sel_pad = -(-n_sel // LANES) * LANES
    msel = _cmp_to_sel_matrix(r, sel_pad)
    i = np.arange(sel_pad)
    utri = jnp.asarray(i[:, None] < i[None, :], BF16)
    cst = lambda a: pl.BlockSpec(a.shape, lambda s, pt: (0,) * a.ndim)
    in_specs = [pl.BlockSpec((1, NSA_HEADS, LANES), lambda s, pt: (s, 0, 0))] + [cst(a) for a in cops] + [cst(msel), cst(utri)]
    in_specs += [pl.BlockSpec((1, rpp, width), lambda s, pt, j=j: (pt[jnp.minimum(s, db - 1), j], 0, 0))
                 for j in range(n_pages)]
    o_c, idx = pl.pallas_call(
        functools.partial(_nsa_sample_cmp_kernel, n_pages=n_pages, past=past, n_cmp=n_cmp, n_sel=n_sel),
        grid_spec=pltpu.PrefetchScalarGridSpec(
            num_scalar_prefetch=1, grid=(db,), in_specs=in_specs,
            out_specs=[pl.BlockSpec((1, NSA_HEADS, LANES), lambda s, pt: (s, 0, 0)),
                       pl.BlockSpec((1, 1, LANES), lambda s, pt: (s, 0, 0))],
            scratch_shapes=[pltpu.VMEM((r, width), F32), pltpu.VMEM((1, NSA_KV_HEADS, r, LANES), BF16),
                            pltpu.VMEM((1, NSA_KV_HEADS, r, LANES), BF16)]),
        out_shape=[jax.ShapeDtypeStruct((db, NSA_HEADS, LANES), F32), jax.ShapeDtypeStruct((db, 1, LANES), jnp.int32)],
        compiler_params=_params(1),
        name="nsa_sample_cmp",
    )(page_table, nqh, *cops, msel, utri, *([cache_cmp.reshape(n_phys, rpp, width)] * n_pages))
    return o_c, idx[:, 0, :NSA_KV_HEADS * TOP_N]


def _nsa_sample_attend_kernel(pt_ref, idx_ref, qmat_ref, oc_ref, sm_ref, snew_ref, snewc_ref, wnew_ref, wnewc_ref,
                              state_ref, *rest, nb_past, bpp):
    del pt_ref
    n_blk = NSA_KV_HEADS * TOP_N
    pages = rest[:n_blk]
    o_ref, wout_ref = rest[n_blk:]
    s = pl.program_id(0)
    g = NSA_GROUP
    qm = qmat_ref[0]
    feat = qm.shape[1]
    row = lax.broadcasted_iota(jnp.int32, (NSA_HEADS, 1), 0)

    def new_score(col_ref):
        return _dot(qm, jnp.broadcast_to(col_ref[0], (feat, LANES)).astype(BF16))[:, 0:1]

    lane_p = _lane((NSA_HEADS, pages[0].shape[2]))
    tiles = []
    has_new = jnp.zeros((NSA_HEADS, 1), jnp.bool_)
    for k in range(NSA_KV_HEADS):
        in_grp = row // g == k
        for n in range(TOP_N):
            idx = idx_ref[s, k * TOP_N + n]
            is_new = idx >= nb_past
            half = jnp.minimum(idx, nb_past - 1) % bpp
            data = pages[k * TOP_N + n][0].astype(BF16)
            ok = in_grp & jnp.logical_not(is_new) & (lane_p // SEL_BLOCK == half)
            tiles.append((jnp.where(ok, _dot(qm, data), NEG), data))
            has_new = has_new | (in_grp & is_new)
    s_new = jnp.where(has_new, new_score(snewc_ref), NEG)
    m = s_new
    for sc, _ in tiles:
        m = jnp.maximum(m, sc.max(axis=-1, keepdims=True))
    l = jnp.exp(s_new - m)
    acc = l * snew_ref[0]
    for sc, data in tiles:
        p = jnp.exp(sc - m)
        l = l + jnp.sum(p, axis=-1, keepdims=True)
        acc = acc + _dot_t(p.astype(BF16), data)
    o_s = acc / l
    st = state_ref[0]
    w_keep = st.shape[1]
    stb = st.astype(BF16)
    sw = _dot(qm, stb)
    sw = jnp.where(_lane(sw.shape) >= w_keep - WINDOW + 1, sw, NEG)
    s_new = new_score(wnewc_ref)
    m = jnp.maximum(jnp.max(sw, axis=-1, keepdims=True), s_new)
    p = jnp.exp(sw - m)
    p_new = jnp.exp(s_new - m)
    o_w = (_dot_t(p.astype(BF16), stb) + p_new * wnew_ref[0]) / (jnp.sum(p, axis=-1, keepdims=True) + p_new)

    def values_of(a):
        v = a[:, LANES:2 * LANES]
        return jnp.where(row < g, v, pltpu.roll(v, HEAD_DIM, 1))

    smb = jnp.broadcast_to(sm_ref[0], (NSA_HEADS, LANES))
    lane = _lane(smb.shape)
    gate = lambda j: jnp.sum(jnp.where(lane == SM_GATE + 3 * row + j, smb, 0.0), axis=-1, keepdims=True)
    o_ref[0] = gate(0) * oc_ref[0] + gate(1) * values_of(o_s) + gate(2) * values_of(o_w)
    lane_f = _lane((feat, LANES))
    n_chunks = w_keep // LANES
    shifted = [pltpu.roll(st[:, c * LANES:(c + 1) * LANES], LANES - 1, 1) for c in range(n_chunks)]
    shifted.append(jnp.broadcast_to(wnewc_ref[0], (feat, LANES)))
    for c in range(n_chunks):
        wout_ref[0, :, c * LANES:(c + 1) * LANES] = jnp.where(lane_f < LANES - 1, shifted[c], shifted[c + 1])


def _nsa_sample_attend(qmat, o_c, small, skv_new, wkv_new, state_t, sel_t, page_table, sel_idx):
    db = qmat.shape[0]
    n_phys, feat, page = sel_t.shape
    bpp = page // SEL_BLOCK
    nb_past = page_table.shape[1] * bpp
    w_keep = state_t.shape[2]
    assert w_keep % LANES == 0
    n_blk = NSA_KV_HEADS * TOP_N

    def page_map(s, pt, idx, n):
        row = jnp.minimum(s, db - 1)
        return (pt[row, jnp.clip(idx[row, n], 0, nb_past - 1) // bpp], 0, 0)

    per_seq = lambda a, b: pl.BlockSpec((1, a, b), lambda s, pt, idx: (s, 0, 0))
    in_specs = [per_seq(NSA_HEADS, feat), per_seq(NSA_HEADS, LANES), per_seq(1, LANES), per_seq(1, feat), per_seq(feat, 1),
                per_seq(1, feat), per_seq(feat, 1), per_seq(feat, w_keep)]
    in_specs += [pl.BlockSpec((1, feat, page), functools.partial(page_map, n=n)) for n in range(n_blk)]
    return pl.pallas_call(
        functools.partial(_nsa_sample_attend_kernel, nb_past=nb_past, bpp=bpp),
        grid_spec=pltpu.PrefetchScalarGridSpec(
            num_scalar_prefetch=2, grid=(db,), in_specs=in_specs,
            out_specs=[per_seq(NSA_HEADS, LANES), per_seq(feat, w_keep)]),
        out_shape=[jax.ShapeDtypeStruct((db, NSA_HEADS, LANES), F32), jax.ShapeDtypeStruct((db, feat, w_keep), F32)],
        compiler_params=_params(1),
        name="nsa_sample_attend",
    )(page_table, sel_idx, qmat, o_c, small.reshape(db, 1, LANES), skv_new.reshape(db, 1, feat), skv_new.reshape(db, feat, 1),
      wkv_new.reshape(db, 1, feat), wkv_new.reshape(db, feat, 1), state_t, *([sel_t] * n_blk))


def kernel(x_prompt, x_sample, cache_fox_kv, cache_fox_logf, cache_cmp_kv, cache_sel_kv, state_win_kv, page_table, g_attn, w_in, b_forget, b_gate, g_q_fox, g_k_fox, g_q_nsa, g_k_cmp, g_k_sel, g_k_win, pos_cmp_k, w_cmp_k1, w_cmp_k2, pos_cmp_v, w_cmp_v1, w_cmp_v2, g_out_fox, g_out_nsa, w_out, g_ffn, w_router_group, b_router_group, w_router_expert, b_router_expert, w_exp_gate, w_exp_up, w_exp_down):
    assert w_in.shape[0] == 1 and x_sample.shape[1] == 1
    l = 0
    proj_ops = _in_proj_operands(g_attn[l], w_in[l], b_forget[l], b_gate[l], g_q_fox[l], g_k_fox[l], g_q_nsa[l],
                                 g_k_sel[l], g_k_win[l])
    cops = _compress_operands(pos_cmp_k[l], w_cmp_k1[l], w_cmp_k2[l], pos_cmp_v[l], w_cmp_v1[l], w_cmp_v2[l], g_k_cmp[l])
    post_ops, moe_ops = _finish_operands(g_out_fox[l], g_out_nsa[l], w_out[l], g_ffn[l], w_router_group[l],
                                         b_router_group[l], w_router_expert[l], b_router_expert[l], w_exp_gate[l],
                                         w_exp_up[l], w_exp_down[l])
    y_p, fkv_p, lf_p, ckv_p, skv_p, wkv_p = _prompt_group(x_prompt, proj_ops, cops, post_ops, moe_ops)
    y_s, fkv_s, lf_s, ckv_s, skv_s, wkv_s = _sample_group(x_sample, cache_fox_kv[l], cache_fox_logf[l], cache_cmp_kv[l],
                                                          cache_sel_kv[l], state_win_kv[l], page_table, proj_ops, cops,
                                                          post_ops, moe_ops)
    return (y_p, y_s, fkv_p, fkv_s, lf_p, lf_s, ckv_p, ckv_s, skv_p, skv_s, wkv_p, wkv_s)


def _prompt_group(x, proj_ops, cops, post_ops, moe_ops):
    b, s, d = x.shape
    tile = _row_tile(s, 512)
    (fkv, ckv, skv, wkv, small, fq, fk, fv, nq, ksel, vsel, kwin, vwin) = _in_proj_prompt(x, proj_ops, tile)
    o_fox = _fox_prompt(fq, fk, fv, tile)
    kc, vc = _compress_prompt(ckv, cops, b)
    o_nsa = _nsa_prompt(nq, kc, vc, ksel, vsel, kwin, vwin, small, 128)
    y = _finish_layer(o_fox.reshape(b * s, FOX_W), o_nsa.reshape(b * s, NSA_W), x.reshape(b * s, d), post_ops, moe_ops,
                      _row_tile(b * s, 1024))
    win = min(WINDOW, s)
    kvh, hd = NSA_KV_HEADS, HEAD_DIM
    return (y.reshape(b, s, d), fkv.reshape(1, b, s, 2, FOX_HEADS, hd),
            small[:, SM_LOGF:SM_LOGF + FOX_HEADS].reshape(1, b, s, FOX_HEADS),
            ckv.reshape(1, b, s, 2, kvh, hd), skv.reshape(1, b, s, 2, kvh, hd),
            wkv.reshape(b, s, 2, kvh, hd)[:, s - win:].reshape(1, b, win, 2, kvh, hd))


def _sample_group(x, cache_fox_kv, cache_fox_logf, cache_cmp_kv, cache_sel_kv, state_win_kv, page_table, proj_ops, cops,
                  post_ops, moe_ops):
    db, _, d = x.shape
    n_phys, page = cache_fox_kv.shape[0], cache_fox_kv.shape[1]
    n_pages = page_table.shape[1]
    past = n_pages * page
    w_keep = state_win_kv.shape[1]
    kvh, hd = NSA_KV_HEADS, HEAD_DIM
    fox_t = jnp.transpose(cache_fox_kv, (0, 2, 3, 4, 1)).reshape(n_phys, 2, FOX_W, page)
    logf_t = jnp.transpose(cache_fox_logf, (0, 2, 1))
    sel_t = jnp.transpose(cache_sel_kv, (0, 2, 3, 4, 1)).reshape(n_phys, 2 * KV_W, page)
    state_t = jnp.transpose(state_win_kv, (0, 2, 3, 4, 1)).reshape(db, 2 * KV_W, w_keep)
    (fkv, skv, wkv, small, fq, nqh, qmat, fkv_t, ckv_t, skv_t, wkv_t, small_t) = _in_proj_sample(x, proj_ops, past)
    w_suffix, tot = _logf_suffix(logf_t)
    pages_per_step = max(t for t in range(1, 17) if n_pages % t == 0)
    o_fox = _fox_sample(fq, fkv, small[:, SM_LOGF:SM_LOGF + FOX_HEADS], fox_t, w_suffix, tot, page_table, pages_per_step)
    o_c, sel_idx = _nsa_sample_cmp(jnp.transpose(nqh, (1, 0, 2)), cache_cmp_kv.reshape(n_phys, page, 2 * KV_W), page_table,
                                   cops, past)
    o_nsa, win_t = _nsa_sample_attend(jnp.transpose(qmat, (1, 0, 2)), o_c, small, skv, wkv, state_t, sel_t, page_table,
                                      sel_idx)
    y = _finish_layer(o_fox, o_nsa[:, :, :HEAD_DIM].reshape(db, NSA_W), x.reshape(db, d), post_ops, moe_ops,
                      _row_tile(db, 1024))
    rows = lambda a_t, *dims: jnp.transpose(a_t.reshape(*dims, db), (len(dims),) + tuple(range(len(dims)))).reshape(
        (1, db, 1) + dims)
    return (y.reshape(db, 1, d), rows(fkv_t, 2, FOX_HEADS, hd), rows(small_t[SM_LOGF:SM_LOGF + FOX_HEADS], FOX_HEADS),
            rows(ckv_t, 2, kvh, hd), rows(skv_t, 2, kvh, hd),
            jnp.transpose(win_t.reshape(db, 2, kvh, hd, w_keep), (0, 4, 1, 2, 3))[None])
```

```python
import functools

import jax
import jax.numpy as jnp
import numpy as np
from jax import lax
from jax.experimental import pallas as pl
from jax.experimental.pallas import tpu as pltpu

F32 = jnp.float32
BF16 = jnp.bfloat16

LANES = 128
HEAD_DIM = 64
FOX_HEADS = 8
NSA_HEADS = 8
NSA_KV_HEADS = 2
NSA_GROUP = NSA_HEADS // NSA_KV_HEADS
ROPE_DIM = HEAD_DIM // 4
ROPE_THETA = 500000.0
CMP_BLOCK = 32
CMP_STRIDE = 16
CMP_HIDDEN = 128
SEL_BLOCK = 64
TOP_N = 8
N_LOCAL = 2
WINDOW = 512
N_GROUPS = 4
EXPERTS_PER_GROUP = 4
N_EXPERTS = N_GROUPS * EXPERTS_PER_GROUP
EPS = 1e-6
NEG = -1e30
FORCE = 1e4
SCALE = HEAD_DIM ** -0.5
PAGE_SIZE = 128

FOX_W = FOX_HEADS * HEAD_DIM
NSA_W = NSA_HEADS * HEAD_DIM
KV_W = NSA_KV_HEADS * HEAD_DIM

C_FQ, C_FK, C_FV, C_NQ = 0, 512, 1024, 1536
C_KC, C_VC, C_KS, C_VS, C_KW, C_VW = 2048, 2176, 2304, 2432, 2560, 2688
C_SMALL = 2816
D_IN_PAD = C_SMALL + LANES
SM_LOGF, SM_GATE, SM_CUM = 0, 8, 32


VMEM_LIMIT = 56 * 1024 * 1024


def _params(grid_rank, vmem=VMEM_LIMIT):
    return pltpu.CompilerParams(dimension_semantics=("arbitrary",) * grid_rank, vmem_limit_bytes=vmem)


def _dot(a, b):
    return jnp.dot(a, b, preferred_element_type=F32)


def _dot_t(a, b):
    return lax.dot_general(a, b, (((1,), (1,)), ((), ())), preferred_element_type=F32)


def _split3(x):
    hi = x.astype(BF16)
    r = x - hi.astype(F32)
    mid = r.astype(BF16)
    lo = (r - mid.astype(F32)).astype(BF16)
    return hi, mid, lo


def _dot3(a_f32, b_bf16):
    hi, mid, lo = _split3(a_f32)
    return _dot(hi, b_bf16) + _dot(mid, b_bf16) + _dot(lo, b_bf16)


def _lane(shape):
    return lax.broadcasted_iota(jnp.int32, shape, len(shape) - 1)


def _head_rms(seg, bd, gain):
    ms = _dot((seg * seg).astype(BF16), bd) * (1.0 / HEAD_DIM)
    return seg * lax.rsqrt(ms + EPS) * gain


def _rope(x, cos, sin_m, sin_p):
    return x * cos + pltpu.roll(x, LANES - ROPE_DIM // 2, 1) * sin_m + pltpu.roll(x, ROPE_DIM // 2, 1) * sin_p


def _in_proj_common(x_ref, gat_ref, w_ref, bsm_ref, gq_fox_ref, gk_fox_ref, gq_nsa_ref, gk_sel_ref, gk_win_ref,
                    bd_ref, cos_ref, sinm_ref, sinp_ref):
    x = x_ref[...]
    xn = (x * lax.rsqrt(jnp.mean(x * x, axis=-1, keepdims=True) + EPS) * gat_ref[...]).astype(BF16)
    bd = bd_ref[...]
    bd128 = bd[:LANES, :LANES]
    cos, sin_m, sin_p = cos_ref[...], sinm_ref[...], sinp_ref[...]

    def proj(c0, width):
        return _dot(xn, w_ref[:, c0:c0 + width])

    def normed(c0, gain_ref, chunks):
        outs = []
        for j in range(chunks):
            seg = proj(c0 + 256 * j, 256)
            outs.append(_head_rms(seg, bd, gain_ref[...]))
        return outs

    out = {}
    out["fq"] = normed(C_FQ, gq_fox_ref, 2)
    out["fk"] = normed(C_FK, gk_fox_ref, 2)
    out["fv"] = [proj(C_FV, 256), proj(C_FV + 256, 256)]
    out["nq"] = normed(C_NQ, gq_nsa_ref, 2)
    out["nqr"] = [jnp.concatenate([_rope(c[:, :LANES], cos, sin_m, sin_p), _rope(c[:, LANES:], cos, sin_m, sin_p)], axis=1)
                  for c in out["nq"]]
    out["ckv"] = proj(C_KC, 256)
    ks = proj(C_KS, LANES)
    out["ks"] = _rope(ks * lax.rsqrt(_dot((ks * ks).astype(BF16), bd128) * (1.0 / HEAD_DIM) + EPS) * gk_sel_ref[...],
                      cos, sin_m, sin_p)
    out["vs"] = proj(C_VS, LANES)
    kw = proj(C_KW, LANES)
    out["kw"] = _rope(kw * lax.rsqrt(_dot((kw * kw).astype(BF16), bd128) * (1.0 / HEAD_DIM) + EPS) * gk_win_ref[...],
                      cos, sin_m, sin_p)
    out["vw"] = proj(C_VW, LANES)
    t = proj(C_SMALL, LANES) + bsm_ref[...]
    logf = jnp.minimum(t, 0.0) - jnp.log1p(jnp.exp(-jnp.abs(t)))
    gate = 1.0 / (1.0 + jnp.exp(-t))
    lane = _lane(t.shape)
    out["small"] = jnp.where(lane < SM_GATE, logf, jnp.where(lane < SM_CUM, gate, 0.0))
    return out


def _in_proj_prompt_kernel(x_ref, gat_ref, w_ref, bsm_ref, gq_fox_ref, gk_fox_ref, gq_nsa_ref, gk_sel_ref, gk_win_ref,
                           bd_ref, cos_ref, sinm_ref, sinp_ref, tri_ref,
                           fkv_ref, ckv_ref, skv_ref, wkv_ref, small_ref,
                           fq_ref, fk_ref, fv_ref, nq_ref, ksel_ref, vsel_ref, kwin_ref, vwin_ref,
                           carry_ref, *, tiles_per_seq):
    p = _in_proj_common(x_ref, gat_ref, w_ref, bsm_ref, gq_fox_ref, gk_fox_ref, gq_nsa_ref, gk_sel_ref, gk_win_ref,
                        bd_ref, cos_ref, sinm_ref, sinp_ref)
    tm = x_ref.shape[0]
    lane = _lane((tm, LANES))
    lo_half = lane < HEAD_DIM

    @pl.when(pl.program_id(0) % tiles_per_seq == 0)
    def _():
        carry_ref[...] = jnp.zeros_like(carry_ref)

    small = p["small"]
    hi, mid, lo = _split3(small)
    tri = tri_ref[...]
    cum = _dot(tri, hi) + _dot(tri, mid) + _dot(tri, lo) + carry_ref[...]
    carry_ref[...] = cum[tm - 1:tm, :]
    cum_sh = pltpu.roll(cum, SM_CUM - SM_LOGF, 1)
    small_ref[...] = jnp.where((lane >= SM_CUM) & (lane < SM_CUM + FOX_HEADS), cum_sh, small)

    fkv_ref[:, 0:256] = p["fk"][0]
    fkv_ref[:, 256:512] = p["fk"][1]
    fkv_ref[:, 512:768] = p["fv"][0]
    fkv_ref[:, 768:1024] = p["fv"][1]
    ckv_ref[...] = p["ckv"]
    skv_ref[:, 0:LANES] = p["ks"]
    skv_ref[:, LANES:2 * LANES] = p["vs"]
    wkv_ref[:, 0:LANES] = p["kw"]
    wkv_ref[:, LANES:2 * LANES] = p["vw"]

    def pair(chunks, j):
        c = chunks[j // 2]
        return c[:, (j % 2) * LANES:(j % 2 + 1) * LANES]

    one = jnp.ones((tm, LANES), F32)
    zero = jnp.zeros((tm, LANES), F32)
    ones_col = jnp.where(lane == HEAD_DIM, one, zero)
    for h in range(FOX_HEADS):
        qc, kc, vc = pair(p["fq"], h // 2), pair(p["fk"], h // 2), pair(p["fv"], h // 2)
        if h % 2:
            qc, kc, vc = pltpu.roll(qc, HEAD_DIM, 1), pltpu.roll(kc, HEAD_DIM, 1), pltpu.roll(vc, HEAD_DIM, 1)
        ch = jnp.broadcast_to(cum[:, SM_LOGF + h:SM_LOGF + h + 1], (tm, LANES))
        c_hi, c_mid, c_lo = _split3(ch)
        c_hi, c_mid, c_lo = c_hi.astype(F32), c_mid.astype(F32), c_lo.astype(F32)
        q_ext = jnp.where(lane == 64, c_hi, jnp.where(lane == 65, c_mid, jnp.where(lane == 66, c_lo,
                          jnp.where(lane < 70, one, zero))))
        k_ext = jnp.where(lane < 67, one, jnp.where(lane == 67, -c_hi, jnp.where(lane == 68, -c_mid,
                          jnp.where(lane == 69, -c_lo, zero))))
        fq_ref[0, h] = jnp.where(lo_half, qc * SCALE, q_ext).astype(BF16)
        fk_ref[0, h] = jnp.where(lo_half, kc, k_ext).astype(BF16)
        fv_ref[0, h] = jnp.where(lo_half, vc, ones_col).astype(BF16)
    for h in range(NSA_HEADS):
        a, r = pair(p["nq"], h // 2), pair(p["nqr"], h // 2)
        if h % 2:
            v = jnp.where(lo_half, pltpu.roll(a, HEAD_DIM, 1), r)
        else:
            v = jnp.where(lo_half, a, pltpu.roll(r, HEAD_DIM, 1))
        nq_ref[0, h] = (v * SCALE).astype(BF16)
    for name_k, name_v, k_ref, v_ref in (("ks", "vs", ksel_ref, vsel_ref), ("kw", "vw", kwin_ref, vwin_ref)):
        kk, vv = p[name_k], p[name_v]
        k_ref[0, 0] = jnp.where(lo_half, zero, pltpu.roll(kk, HEAD_DIM, 1)).astype(BF16)
        k_ref[0, 1] = jnp.where(lo_half, zero, kk).astype(BF16)
        v_ref[0, 0] = jnp.where(lo_half, vv, ones_col).astype(BF16)
        v_ref[0, 1] = jnp.where(lo_half, pltpu.roll(vv, HEAD_DIM, 1), ones_col).astype(BF16)


def _in_proj_sample_kernel(x_ref, gat_ref, w_ref, bsm_ref, gq_fox_ref, gk_fox_ref, gq_nsa_ref, gk_sel_ref, gk_win_ref,
                           bd_ref, cos_ref, sinm_ref, sinp_ref,
                           fkv_ref, skv_ref, wkv_ref, small_ref, fq_ref, nqh_ref, qmat_ref,
                           fkv_t_ref, ckv_t_ref, skv_t_ref, wkv_t_ref, small_t_ref):
    p = _in_proj_common(x_ref, gat_ref, w_ref, bsm_ref, gq_fox_ref, gk_fox_ref, gq_nsa_ref, gk_sel_ref, gk_win_ref,
                        bd_ref, cos_ref, sinm_ref, sinp_ref)
    small_ref[...] = p["small"]
    small_t_ref[...] = p["small"].T
    for j in range(2):
        fkv_ref[:, 256 * j:256 * (j + 1)] = p["fk"][j]
        fkv_ref[:, 512 + 256 * j:512 + 256 * (j + 1)] = p["fv"][j]
        fkv_t_ref[256 * j:256 * (j + 1), :] = p["fk"][j].T
        fkv_t_ref[512 + 256 * j:512 + 256 * (j + 1), :] = p["fv"][j].T
        fq_ref[:, 256 * j:256 * (j + 1)] = p["fq"][j] * SCALE
    ckv_t_ref[...] = p["ckv"].T
    for ref, ref_t, k_name, v_name in ((skv_ref, skv_t_ref, "ks", "vs"), (wkv_ref, wkv_t_ref, "kw", "vw")):
        ref[:, 0:LANES] = p[k_name]
        ref[:, LANES:2 * LANES] = p[v_name]
        ref_t[0:LANES, :] = p[k_name].T
        ref_t[LANES:2 * LANES, :] = p[v_name].T
    tm = x_ref.shape[0]
    lo_half = _lane((tm, LANES)) < HEAD_DIM
    zero = jnp.zeros((tm, LANES), F32)
    for h in range(NSA_HEADS):
        a = p["nq"][h // 4][:, (h // 2 % 2) * LANES:(h // 2 % 2 + 1) * LANES]
        r = p["nqr"][h // 4][:, (h // 2 % 2) * LANES:(h // 2 % 2 + 1) * LANES]
        if h % 2:
            v = jnp.where(lo_half, pltpu.roll(a, HEAD_DIM, 1), r)
        else:
            v = jnp.where(lo_half, a, pltpu.roll(r, HEAD_DIM, 1))
        nqh_ref[h] = (v * SCALE).astype(BF16)
        kvh = h // NSA_GROUP
        rr = r if (h % 2) == kvh else pltpu.roll(r, HEAD_DIM, 1)
        rr = jnp.where(lo_half, rr, zero) if kvh == 0 else jnp.where(lo_half, zero, rr)
        qmat_ref[h, :, 0:LANES] = (rr * SCALE).astype(BF16)
        qmat_ref[h, :, LANES:2 * LANES] = zero.astype(BF16)


def _const_spec(shape):
    nd = len(shape)
    return pl.BlockSpec(shape, lambda *_: (0,) * nd)


def _rope_tables(pos):
    half = ROPE_DIM // 2
    inv = ROPE_THETA ** (-jnp.arange(half, dtype=F32) * 2.0 / ROPE_DIM)
    ang = pos.astype(F32)[:, None] * inv[None, :]
    cos, sin = jnp.cos(ang), jnp.sin(ang)
    n = pos.shape[0]
    rest = HEAD_DIM - ROPE_DIM
    cos_h = jnp.concatenate([cos, cos, jnp.ones((n, rest), F32)], axis=1)
    sinm_h = jnp.concatenate([-sin, jnp.zeros((n, half + rest), F32)], axis=1)
    sinp_h = jnp.concatenate([jnp.zeros((n, half), F32), sin, jnp.zeros((n, rest), F32)], axis=1)
    return tuple(jnp.concatenate([t, t], axis=1) for t in (cos_h, sinm_h, sinp_h))


def _in_proj_operands(g_attn, w_in, b_forget, b_gate, g_q_fox, g_k_fox, g_q_nsa, g_k_sel, g_k_win):
    d_model = w_in.shape[0]
    sizes = (FOX_W, FOX_W, FOX_W, FOX_HEADS, NSA_W, KV_W, KV_W, KV_W, KV_W, KV_W, KV_W, 3 * NSA_HEADS)
    offs = np.concatenate([[0], np.cumsum(sizes)])
    fq, fk, fv, ff, nq, kc, vc, ks, vs, kw, vw, ng = [w_in[:, offs[i]:offs[i + 1]] for i in range(len(sizes))]
    pad = jnp.zeros((d_model, LANES - FOX_HEADS - 3 * NSA_HEADS), w_in.dtype)
    w = jnp.concatenate([fq, fk, fv, nq, kc, vc, ks, vs, kw, vw, ff, ng, pad], axis=1).astype(BF16)
    bsm = jnp.concatenate([b_forget, b_gate, jnp.zeros((LANES - FOX_HEADS - 3 * NSA_HEADS,), F32)]).reshape(1, LANES)
    hd = np.arange(256) // HEAD_DIM
    bd = jnp.asarray(hd[:, None] == hd[None, :], BF16)
    t256 = lambda g: jnp.tile(g, 256 // HEAD_DIM).reshape(1, 256)
    t128 = lambda g: jnp.tile(g, LANES // HEAD_DIM).reshape(1, LANES)
    return (g_attn.reshape(1, d_model), w, bsm, t256(g_q_fox), t256(g_k_fox), t256(g_q_nsa), t128(g_k_sel), t128(g_k_win), bd)


def _in_proj_prompt(x, ops, tm):
    b, s, d = x.shape
    n = b * s
    tps = s // tm
    cos, sinm, sinp = _rope_tables(jnp.arange(s, dtype=jnp.int32))
    tri = jnp.asarray(np.tril(np.ones((tm, tm), np.float32)), BF16)
    tab_spec = pl.BlockSpec((tm, LANES), lambda i: (i % tps, 0))
    in_specs = ([pl.BlockSpec((tm, d), lambda i: (i, 0))] + [_const_spec(o.shape) for o in ops]
                + [tab_spec, tab_spec, tab_spec, _const_spec((tm, tm))])
    row = lambda w: pl.BlockSpec((tm, w), lambda i: (i, 0))
    head = lambda nh: pl.BlockSpec((1, nh, tm, LANES), lambda i: (i // tps, 0, i % tps, 0))
    hshape = lambda nh: jax.ShapeDtypeStruct((b, nh, s, LANES), BF16)
    out_specs = [row(2 * FOX_W), row(2 * KV_W), row(2 * KV_W), row(2 * KV_W), row(LANES),
                 head(FOX_HEADS), head(FOX_HEADS), head(FOX_HEADS), head(NSA_HEADS),
                 head(NSA_KV_HEADS), head(NSA_KV_HEADS), head(NSA_KV_HEADS), head(NSA_KV_HEADS)]
    out_shape = [jax.ShapeDtypeStruct((n, 2 * FOX_W), F32)] + [jax.ShapeDtypeStruct((n, 2 * KV_W), F32)] * 3 + [
        jax.ShapeDtypeStruct((n, LANES), F32), hshape(FOX_HEADS), hshape(FOX_HEADS), hshape(FOX_HEADS), hshape(NSA_HEADS),
        hshape(NSA_KV_HEADS), hshape(NSA_KV_HEADS), hshape(NSA_KV_HEADS), hshape(NSA_KV_HEADS)]
    return pl.pallas_call(
        functools.partial(_in_proj_prompt_kernel, tiles_per_seq=tps),
        grid=(n // tm,), in_specs=in_specs, out_specs=out_specs, out_shape=out_shape,
        scratch_shapes=[pltpu.VMEM((1, LANES), F32)],
        compiler_params=_params(1),
        name="in_proj_prompt",
    )(x.reshape(n, d), *ops, cos, sinm, sinp, tri)


def _in_proj_sample(x, ops, past):
    b, q, d = x.shape
    n = b * q
    pos = past + jnp.arange(q, dtype=jnp.int32)
    cos, sinm, sinp = (jnp.tile(t, (b, 1)) for t in _rope_tables(pos))
    full = lambda shape: pl.BlockSpec(shape, lambda i: (0, 0))
    in_specs = [full((n, d))] + [_const_spec(o.shape) for o in ops] + [full((n, LANES))] * 3
    widths = (2 * FOX_W, 2 * KV_W, 2 * KV_W, LANES, FOX_W)
    widths_t = (2 * FOX_W, 2 * KV_W, 2 * KV_W, 2 * KV_W, LANES)
    hm = lambda w: pl.BlockSpec((NSA_HEADS, n, w), lambda i: (0, 0, 0))
    return pl.pallas_call(
        _in_proj_sample_kernel, grid=(1,), in_specs=in_specs,
        out_specs=[full((n, w)) for w in widths] + [hm(LANES), hm(2 * LANES)] + [full((w, n)) for w in widths_t],
        out_shape=[jax.ShapeDtypeStruct((n, w), F32) for w in widths] + [
            jax.ShapeDtypeStruct((NSA_HEADS, n, LANES), BF16), jax.ShapeDtypeStruct((NSA_HEADS, n, 2 * LANES), BF16)] + [
            jax.ShapeDtypeStruct((w, n), F32) for w in widths_t],
        compiler_params=_params(1),
        name="in_proj_sample",
    )(x.reshape(n, d), *ops, cos, sinm, sinp)


def _flash_step(q, k, v, m, acc, mask=None):
    s = _dot_t(q, k)
    if mask is not None:
        s = jnp.where(mask, s, NEG)
    m_new = jnp.maximum(m, jnp.max(s, axis=-1, keepdims=True))
    p = jnp.exp(s - m_new)
    acc = acc * jnp.exp(m - m_new) + _dot(p.astype(BF16), v)
    return m_new, acc


def _finish(acc):
    return acc / acc[:, HEAD_DIM:HEAD_DIM + 1]


def _pack_heads(a, b):
    return jnp.where(_lane(a.shape) < HEAD_DIM, a, pltpu.roll(b, HEAD_DIM, 1))


def _fox_prompt_kernel(q_ref, k_ref, v_ref, o_ref, *, tile):
    qi = pl.program_id(2)
    rowi = lax.broadcasted_iota(jnp.int32, (tile, tile), 0)
    coli = lax.broadcasted_iota(jnp.int32, (tile, tile), 1)
    outs = []
    for j in range(2):
        q = q_ref[0, j]

        def body(t, carry, j=j, q=q):
            off = pl.multiple_of(t * tile, tile)
            return _flash_step(q, k_ref[0, j, pl.ds(off, tile), :], v_ref[0, j, pl.ds(off, tile), :], *carry)

        init = (jnp.full((tile, 1), NEG, F32), jnp.zeros((tile, LANES), F32))
        m, acc = lax.fori_loop(0, qi, body, init)
        off = pl.multiple_of(qi * tile, tile)
        m, acc = _flash_step(q, k_ref[0, j, pl.ds(off, tile), :], v_ref[0, j, pl.ds(off, tile), :], m, acc,
                             mask=coli <= rowi)
        outs.append(_finish(acc))
    o_ref[0] = _pack_heads(outs[0], outs[1])


def _fox_prompt(fq, fk, fv, tile):
    b, h, s, _ = fq.shape
    qspec = pl.BlockSpec((1, 2, tile, LANES), lambda bi, hi, qi: (bi, hi, qi, 0))
    kspec = pl.BlockSpec((1, 2, s, LANES), lambda bi, hi, qi: (bi, hi, 0, 0))
    return pl.pallas_call(
        functools.partial(_fox_prompt_kernel, tile=tile),
        grid=(b, h // 2, s // tile), in_specs=[qspec, kspec, kspec],
        out_specs=pl.BlockSpec((1, tile, LANES), lambda bi, hi, qi: (bi, qi, hi)),
        out_shape=jax.ShapeDtypeStruct((b, s, h * HEAD_DIM), F32),
        compiler_params=_params(3),
        name="fox_prompt",
    )(fq, fk, fv)


def _compress_operands(pos_k, w_k1, w_k2, pos_v, w_v1, w_v2, g_k_cmp):
    half = CMP_BLOCK // 2
    wk = jnp.stack([w_k1, w_k1, w_v1, w_v1]).reshape(4, 2, half, HEAD_DIM, CMP_HIDDEN)
    wbig = jnp.einsum("kardh,kj->rkdajh", wk, jnp.eye(4, dtype=w_k1.dtype))
    wbig = wbig.reshape(half * 4 * HEAD_DIM, 2 * 4 * CMP_HIDDEN).astype(BF16)
    pos = jnp.stack([pos_k, pos_k, pos_v, pos_v]).reshape(4, 2, half, HEAD_DIM)
    pos_rows = jnp.transpose(pos, (1, 2, 0, 3)).reshape(2, half * 4 * HEAD_DIM)
    pos_rows = jnp.concatenate([pos_rows, jnp.zeros((6, pos_rows.shape[1]), pos_rows.dtype)]).astype(BF16)
    pad2 = lambda w: jnp.pad(w, ((0, 0), (0, LANES - HEAD_DIM))).astype(BF16)
    gain = jnp.pad(g_k_cmp, (0, LANES - HEAD_DIM)).reshape(1, LANES)
    return wbig, pos_rows, pad2(w_k2), pad2(w_v2), gain


def _compress_core(x, w_ref, pos_ref, w2k_ref, w2v_ref, gain_ref, kc_ref, vc_ref):
    _compress_finish(_dot(x.astype(BF16), w_ref[...]), w_ref, pos_ref, w2k_ref, w2v_ref, gain_ref, kc_ref, vc_ref)


def _compress_finish(ab, w_ref, pos_ref, w2k_ref, w2v_ref, gain_ref, kc_ref, vc_ref):
    r = ab.shape[0]
    hw = 4 * CMP_HIDDEN
    pw = _dot(pos_ref[...], w_ref[...])
    hid = ab[:, :hw] + pltpu.roll(ab[:, hw:], r - 1, 0) + pw[0:1, :hw] + pw[1:2, hw:]
    hid = (hid * (1.0 / (1.0 + jnp.exp(-hid)))).astype(BF16)
    lane = _lane((r, LANES))
    for kvh in range(NSA_KV_HEADS):
        kk = _dot(hid[:, kvh * CMP_HIDDEN:(kvh + 1) * CMP_HIDDEN], w2k_ref[...])
        ms = jnp.sum(kk * kk, axis=-1, keepdims=True) * (1.0 / HEAD_DIM)
        kc_ref[0, kvh] = (kk * lax.rsqrt(ms + EPS) * gain_ref[...]).astype(BF16)
        vv = _dot(hid[:, (2 + kvh) * CMP_HIDDEN:(3 + kvh) * CMP_HIDDEN], w2v_ref[...])
        vc_ref[0, kvh] = jnp.where(lane == HEAD_DIM, 1.0, vv).astype(BF16)


def _compress_prompt_kernel(x_ref, w_ref, pos_ref, w2k_ref, w2v_ref, gain_ref, kc_ref, vc_ref):
    _compress_core(x_ref[0], w_ref, pos_ref, w2k_ref, w2v_ref, gain_ref, kc_ref, vc_ref)


def _compress_prompt(ckv, cops, b):
    n = ckv.shape[0]
    r = n // b // CMP_STRIDE
    width = CMP_STRIDE * ckv.shape[1]
    out = jax.ShapeDtypeStruct((b, NSA_KV_HEADS, r, LANES), BF16)
    ospec = pl.BlockSpec((1, NSA_KV_HEADS, r, LANES), lambda i: (i, 0, 0, 0))
    return pl.pallas_call(
        _compress_prompt_kernel, grid=(b,),
        in_specs=[pl.BlockSpec((1, r, width), lambda i: (i, 0, 0))] + [_const_spec(o.shape) for o in cops],
        out_specs=[ospec, ospec], out_shape=[out, out],
        compiler_params=_params(1),
        name="compress_prompt",
    )(ckv.reshape(b, r, width), *cops)


def _cmp_branch(q, kc, vc, qpos_rows, n_valid_limit):
    s = _dot_t(q, kc)
    cidx = _lane(s.shape)
    valid = (cidx * CMP_STRIDE + (CMP_BLOCK - 1) <= qpos_rows) & (cidx < n_valid_limit)
    s = jnp.where(valid, s, NEG)
    p = jnp.exp(s - jnp.max(s, axis=-1, keepdims=True)) * valid.astype(F32)
    l = jnp.sum(p, axis=-1, keepdims=True)
    p = p / jnp.where(l > 0, l, 1.0)
    return _dot(p.astype(BF16), vc), p


def _select_blocks(imp, qblk, n_sel):
    j = _lane(imp.shape)
    dist = qblk - j
    forced = (j == 0) | ((dist >= 0) & (dist < N_LOCAL))
    score = jnp.where(dist >= 0, jnp.where(forced, FORCE, imp), NEG)
    score = jnp.where(j < n_sel, score, -jnp.inf)
    cnt = jnp.zeros(imp.shape, F32)
    for i in range(n_sel):
        col = score[:, i:i + 1]
        beats = (col > score) | ((col == score) & (j > i))
        cnt = cnt + beats.astype(F32)
    return (cnt < min(TOP_N, n_sel)) & (j < n_sel)


def _select_blocks_t(imp_t, qblk, n_sel):
    j = lax.broadcasted_iota(jnp.int32, imp_t.shape, 0)
    dist = qblk - j
    forced = (j == 0) | ((dist >= 0) & (dist < N_LOCAL))
    score = jnp.where(dist >= 0, jnp.where(forced, FORCE, imp_t), NEG)
    cnt = jnp.zeros(imp_t.shape, F32)
    for i in range(n_sel):
        row = jnp.broadcast_to(score[i:i + 1, :], score.shape)
        beats = (row > score) | ((row == score) & (j > i))
        cnt = cnt + beats.astype(F32)
    return cnt < min(TOP_N, n_sel)


def _nsa_prompt_kernel(q_ref, kc_ref, vc_ref, ks_ref, vs_ref, kw_ref, vw_ref, sm_ref, msel_ref, exp_ref, o_ref, *,
                       tq, tk, n_cmp, n_sel):
    i = pl.program_id(1)
    start = i * tq
    g = NSA_GROUP
    rows = g * tq
    s_len = ks_ref.shape[2]
    qpos = start + lax.broadcasted_iota(jnp.int32, (tq, 1), 0)
    qpos_rows = jnp.concatenate([qpos] * g, axis=0)
    qblk_t = (start + lax.broadcasted_iota(jnp.int32, (1, tq), 1)) // SEL_BLOCK
    small = sm_ref[...]
    kpos_l = lax.broadcasted_iota(jnp.int32, (tq, tk), 1)
    band = min(WINDOW + tq, s_len)
    kpos_w = lax.broadcasted_iota(jnp.int32, (tq, band), 1)
    init = (jnp.full((rows, 1), NEG, F32), jnp.zeros((rows, LANES), F32))
    heads = []
    for k in range(NSA_KV_HEADS):
        q = q_ref[0, k * g:(k + 1) * g].reshape(rows, LANES)
        o_c, p = _cmp_branch(q, kc_ref[0, k], vc_ref[0, k], qpos_rows, n_cmp)
        psum = p[0:tq]
        for gi in range(1, g):
            psum = psum + p[gi * tq:(gi + 1) * tq]
        hi = psum.astype(BF16)
        lo = (psum - hi.astype(F32)).astype(BF16)
        imp = _dot(hi, msel_ref[...]) + _dot(lo, msel_ref[...])
        sel_t = _select_blocks_t(imp.T[0:n_sel], qblk_t, n_sel)
        sel_t = jnp.concatenate([sel_t.astype(F32), jnp.zeros((LANES - n_sel, tq), F32)], axis=0)
        selb = sel_t.T.astype(BF16)

        def sel_body(t, carry, k=k, q=q, selb=selb):
            off = pl.multiple_of(t * tk, tk)
            picked = _dot(selb, exp_ref[t]) > 0.5
            ok = picked & (t * tk + kpos_l <= qpos)
            s = _dot_t(q, ks_ref[0, k, pl.ds(off, tk), :]).reshape(g, tq, tk)
            s = jnp.where(ok[None], s, NEG).reshape(rows, tk)
            m, acc = carry
            m_new = jnp.maximum(m, jnp.max(s, axis=-1, keepdims=True))
            pr = jnp.exp(s - m_new)
            acc = acc * jnp.exp(m - m_new) + _dot(pr.astype(BF16), vs_ref[0, k, pl.ds(off, tk), :])
            return m_new, acc

        _, acc_s = lax.fori_loop(0, (start + tq + tk - 1) // tk, sel_body, init)
        off = pl.multiple_of(jnp.maximum(start + tq - band, 0), tq)
        diff = qpos - (off + kpos_w)
        ok = (diff >= 0) & (diff < WINDOW)
        s = _dot_t(q, kw_ref[0, k, pl.ds(off, band), :]).reshape(g, tq, band)
        s = jnp.where(ok[None], s, NEG).reshape(rows, band)
        pr = jnp.exp(s - jnp.max(s, axis=-1, keepdims=True))
        acc_w = _dot(pr.astype(BF16), vw_ref[0, k, pl.ds(off, band), :])
        o_s, o_w = _finish(acc_s), _finish(acc_w)
        for gi in range(g):
            h = k * g + gi
            sl = slice(gi * tq, (gi + 1) * tq)
            c0 = SM_GATE + 3 * h
            heads.append(small[:, c0:c0 + 1] * o_c[sl] + small[:, c0 + 1:c0 + 2] * o_s[sl] + small[:, c0 + 2:c0 + 3] * o_w[sl])
    for c in range(NSA_HEADS // 2):
        o_ref[0, :, c * LANES:(c + 1) * LANES] = _pack_heads(heads[2 * c], heads[2 * c + 1])


def _cmp_to_sel_matrix(n_rows, n_sel):
    lead = CMP_BLOCK // CMP_STRIDE - 1
    r = SEL_BLOCK // CMP_STRIDE
    c = np.arange(n_rows)[:, None]
    j = np.arange(n_sel)[None, :]
    return jnp.asarray((c >= r * j - lead) & (c <= r * j + r - 1), BF16)


def _nsa_prompt(nq, kc, vc, ksel, vsel, kwin, vwin, small, tq):
    b, h, s, _ = nq.shape
    nb = s // tq
    r = kc.shape[2]
    n_cmp = (s - CMP_BLOCK) // CMP_STRIDE + 1
    n_sel = s // SEL_BLOCK
    assert n_sel <= LANES
    tk = _row_tile(s, 512)
    msel = _cmp_to_sel_matrix(r, LANES)
    key_blk = (np.arange(s) // SEL_BLOCK).reshape(s // tk, 1, tk)
    expand = jnp.asarray(np.arange(LANES).reshape(1, LANES, 1) == key_blk, BF16)
    seq_spec = lambda rows: pl.BlockSpec((1, NSA_KV_HEADS, rows, LANES), lambda bi, qi: (bi, 0, 0, 0))
    return pl.pallas_call(
        functools.partial(_nsa_prompt_kernel, tq=tq, tk=tk, n_cmp=n_cmp, n_sel=n_sel),
        grid=(b, nb),
        in_specs=[pl.BlockSpec((1, h, tq, LANES), lambda bi, qi: (bi, 0, qi, 0)), seq_spec(r), seq_spec(r),
                  seq_spec(s), seq_spec(s), seq_spec(s), seq_spec(s),
                  pl.BlockSpec((tq, LANES), lambda bi, qi: (bi * nb + qi, 0)), _const_spec(msel.shape),
                  _const_spec(expand.shape)],
        out_specs=pl.BlockSpec((1, tq, h * HEAD_DIM), lambda bi, qi: (bi, qi, 0)),
        out_shape=jax.ShapeDtypeStruct((b, s, h * HEAD_DIM), F32),
        compiler_params=_params(2),
        name="nsa_prompt",
    )(nq, kc, vc, ksel, vsel, kwin, vwin, small, msel, expand)


def _rms(x, gain):
    return x * lax.rsqrt(jnp.mean(x * x, axis=-1, keepdims=True) + EPS) * gain


def _first_lane_where(cond, lane):
    return jnp.min(jnp.where(cond, lane, LANES), axis=-1, keepdims=True)


def _post_attn_kernel(of_ref, on_ref, x_ref, gf_ref, gn_ref, wo_ref, gffn_ref, wr_hi_ref, wr_lo_ref, br_ref,
                      x1_ref, h_ref, gate_ref):
    nf = _rms(of_ref[...], gf_ref[...]).astype(BF16)
    nn = _rms(on_ref[...], gn_ref[...]).astype(BF16)
    x1 = x_ref[...] + _dot(nf, wo_ref[0:FOX_W, :]) + _dot(nn, wo_ref[FOX_W:FOX_W + NSA_W, :])
    x1_ref[...] = x1
    h = _rms(x1, gffn_ref[...])
    h_ref[...] = h.astype(BF16)
    h_hi = h.astype(BF16)
    h_lo = (h - h_hi.astype(F32)).astype(BF16)
    logits = _dot(h_hi, wr_hi_ref[...]) + _dot(h_lo, wr_hi_ref[...]) + _dot(h_hi, wr_lo_ref[...]) + br_ref[...]
    lane = _lane(logits.shape)
    is_grp = (lane >= N_EXPERTS) & (lane < N_EXPERTS + N_GROUPS)
    lg = jnp.where(is_grp, logits, NEG)
    lg_max = jnp.max(lg, axis=-1, keepdims=True)
    gsel = _first_lane_where(is_grp & (lg == lg_max), lane) - N_EXPERTS
    pg_sel = 1.0 / jnp.sum(jnp.where(is_grp, jnp.exp(lg - lg_max), 0.0), axis=-1, keepdims=True)
    in_grp = (lane >= gsel * EXPERTS_PER_GROUP) & (lane < (gsel + 1) * EXPERTS_PER_GROUP)
    le = jnp.where(in_grp, logits, NEG)
    pe = jnp.where(in_grp, jnp.exp(le - jnp.max(le, axis=-1, keepdims=True)), 0.0)
    pe = pe / jnp.sum(pe, axis=-1, keepdims=True)
    pe = jnp.where(in_grp, pe, -1.0)
    v1 = jnp.max(pe, axis=-1, keepdims=True)
    i1 = _first_lane_where(pe == v1, lane)
    rest = jnp.where(lane == i1, -1.0, pe)
    v2 = jnp.max(rest, axis=-1, keepdims=True)
    i2 = _first_lane_where(rest == v2, lane)
    tot = v1 + v2
    gate_ref[...] = jnp.where(lane == i1, v1 / tot * pg_sel, jnp.where(lane == i2, v2 / tot * pg_sel, 0.0))


def _moe_kernel(h_ref, gate_ref, x1_ref, wgu_ref, wd_ref, y_ref):
    e = pl.program_id(1)

    @pl.when(e == 0)
    def _():
        y_ref[...] = x1_ref[...]

    d_exp = wd_ref.shape[1]
    gu = _dot(h_ref[...], wgu_ref[0])
    gpre, up = gu[:, :d_exp], gu[:, d_exp:]
    gate = gate_ref[...]
    ge = jnp.sum(jnp.where(_lane(gate.shape) == e, gate, 0.0), axis=-1, keepdims=True)
    hid = gpre * (1.0 / (1.0 + jnp.exp(-gpre))) * up * ge
    y_ref[...] += _dot(hid.astype(BF16), wd_ref[0])


def _finish_operands(g_out_fox, g_out_nsa, w_out, g_ffn, w_rg, b_rg, w_re, b_re, w_gate, w_up, w_down):
    d = w_out.shape[1]
    pad = LANES - N_EXPERTS - N_GROUPS
    wr = jnp.concatenate([w_re, w_rg, jnp.zeros((d, pad), F32)], axis=1)
    wr_hi = wr.astype(BF16)
    wr_lo = (wr - wr_hi.astype(F32)).astype(BF16)
    br = jnp.concatenate([b_re, b_rg, jnp.zeros((pad,), F32)]).reshape(1, LANES)
    post = (g_out_fox.reshape(1, -1), g_out_nsa.reshape(1, -1), w_out.astype(BF16), g_ffn.reshape(1, d), wr_hi, wr_lo, br)
    moe = (jnp.concatenate([w_gate, w_up], axis=-1).astype(BF16), w_down.astype(BF16))
    return post, moe


def _finish_layer(o_fox, o_nsa, x, post_ops, moe_ops, tm):
    n, d = x.shape
    tp = _row_tile(n, 512)
    row = lambda w: pl.BlockSpec((tp, w), lambda i: (i, 0))
    x1, h, gate = pl.pallas_call(
        _post_attn_kernel, grid=(n // tp,),
        in_specs=[row(FOX_W), row(NSA_W), row(d)] + [_const_spec(o.shape) for o in post_ops],
        out_specs=[row(d), row(d), row(LANES)],
        out_shape=[jax.ShapeDtypeStruct((n, d), F32), jax.ShapeDtypeStruct((n, d), BF16), jax.ShapeDtypeStruct((n, LANES), F32)],
        compiler_params=_params(1),
        name="post_attn",
    )(o_fox, o_nsa, x, *post_ops)
    wgu, wd = moe_ops
    tok = lambda w: pl.BlockSpec((tm, w), lambda i, e: (i, 0))
    return pl.pallas_call(
        _moe_kernel, grid=(n // tm, wgu.shape[0]),
        in_specs=[tok(d), tok(LANES), tok(d),
                  pl.BlockSpec((1,) + wgu.shape[1:], lambda i, e: (e, 0, 0)),
                  pl.BlockSpec((1,) + wd.shape[1:], lambda i, e: (e, 0, 0))],
        out_specs=tok(d), out_shape=jax.ShapeDtypeStruct((n, d), F32),
        compiler_params=_params(2),
        name="moe",
    )(h, gate, x1, wgu, wd)


def _row_tile(n, cap):
    for t in range(min(cap, n), 7, -1):
        if n % t == 0 and t % 8 == 0:
            return t
    return n


def _logf_suffix_kernel(lf_ref, u_ref, ones_ref, w_ref, tot_ref):
    hi, mid, lo = _split3(lf_ref[...])
    u, o = u_ref[...], ones_ref[...]
    w_ref[...] = _dot(hi, u) + _dot(mid, u) + _dot(lo, u)
    tot_ref[...] = _dot(hi, o) + _dot(mid, o) + _dot(lo, o)


def _logf_suffix(logf_t):
    n_phys, nh, page = logf_t.shape
    rows = n_phys * nh
    t = np.arange(page)
    u = jnp.asarray(t[:, None] > t[None, :], BF16)
    ones = jnp.ones((page, page), BF16)
    tr = _row_tile(rows, 2048)
    spec = pl.BlockSpec((tr, page), lambda i: (i, 0))
    out = jax.ShapeDtypeStruct((rows, page), F32)
    w, tot = pl.pallas_call(
        _logf_suffix_kernel, grid=(rows // tr,),
        in_specs=[spec, _const_spec(u.shape), _const_spec(ones.shape)],
        out_specs=[spec, spec], out_shape=[out, out],
        compiler_params=_params(1),
        name="logf_suffix",
    )(logf_t.reshape(rows, page), u, ones)
    return w.reshape(n_phys, nh, page), tot.reshape(n_phys, nh, page)


def _per_head(col):
    nh, w = col.shape
    return jnp.broadcast_to(col[:, None, :], (nh, HEAD_DIM, w)).reshape(nh * HEAD_DIM, w)


def _head_sums(x):
    return jnp.sum(x.reshape(x.shape[0] // HEAD_DIM, HEAD_DIM, x.shape[1]), axis=1)


def _fox_sample_kernel(pt_ref, q_ref, new_ref, lfn_ref, *rest, pc):
    del pt_ref
    kv_refs, w_refs, tot_refs = rest[:pc], rest[pc:2 * pc], rest[2 * pc:3 * pc]
    o_ref = rest[3 * pc]
    qb_ref, m_ref, carry_ref, acc_ref, l_ref = rest[3 * pc + 1:]
    c = pl.program_id(1)
    page = acc_ref.shape[1]

    @pl.when(c == 0)
    def _():
        qb_ref[...] = jnp.broadcast_to(q_ref[0], qb_ref.shape)
        m_ref[...] = jnp.full(m_ref.shape, NEG, F32)
        carry_ref[...] = jnp.zeros_like(carry_ref)
        acc_ref[...] = jnp.zeros_like(acc_ref)
        l_ref[...] = jnp.zeros_like(l_ref)

    qb = qb_ref[...]
    carry = carry_ref[...]
    scores = []
    for j in range(pc):
        scores.append(_head_sums(kv_refs[j][0, 0] * qb) + (lfn_ref[0] + carry + w_refs[j][0]))
        carry = carry + tot_refs[j][0]
    carry_ref[...] = carry
    m_chunk = scores[0]
    for s in scores[1:]:
        m_chunk = jnp.maximum(m_chunk, s)
    m_old = m_ref[...]
    m_new = jnp.maximum(m_old, jnp.max(m_chunk, axis=-1, keepdims=True))
    m_ref[...] = m_new
    alpha = jnp.exp(m_old - m_new)
    acc = acc_ref[...] * _per_head(alpha)
    l = l_ref[...] * alpha
    for j in range(pc):
        p = jnp.exp(scores[j] - m_new)
        l = l + p
        acc = acc + _per_head(p) * kv_refs[j][0, 1]
    acc_ref[...] = acc
    l_ref[...] = l

    @pl.when(c == pl.num_programs(1) - 1)
    def _():
        width = qb.shape[0]
        k_new, v_new = new_ref[0, 0:width, :], new_ref[0, width:2 * width, :]
        s_new = _head_sums(k_new * q_ref[0])
        m_fin = jnp.maximum(m_new, s_new)
        a = jnp.exp(m_new - m_fin)
        p_new = jnp.exp(s_new - m_fin)
        l_tot = jnp.sum(l, axis=-1, keepdims=True) * a + p_new
        acc_col = jnp.sum(acc, axis=-1, keepdims=True) * _per_head(a) + _per_head(p_new) * v_new
        o_ref[0] = acc_col / _per_head(l_tot)


def _fox_sample(fq, fkv_new, lf_new, kv_t, w_suffix, tot, page_table, pc):
    db = fq.shape[0]
    n_phys, _, width, page = kv_t.shape
    nh = w_suffix.shape[1]
    n_pages = page_table.shape[1]
    phys = lambda s, c, pt, j: pt[jnp.minimum(s, db - 1), jnp.clip(n_pages - 1 - (c * pc + j), 0, n_pages - 1)]
    per_seq = lambda r: pl.BlockSpec((1, r, 1), lambda s, c, pt: (s, 0, 0))
    in_specs = [per_seq(width), per_seq(2 * width), per_seq(nh)]
    in_specs += [pl.BlockSpec((1, 2, width, page), lambda s, c, pt, j=j: (phys(s, c, pt, j), 0, 0, 0)) for j in range(pc)]
    in_specs += [pl.BlockSpec((1, nh, page), lambda s, c, pt, j=j: (phys(s, c, pt, j), 0, 0)) for j in list(range(pc)) * 2]
    out = pl.pallas_call(
        functools.partial(_fox_sample_kernel, pc=pc),
        grid_spec=pltpu.PrefetchScalarGridSpec(
            num_scalar_prefetch=1, grid=(db, n_pages // pc), in_specs=in_specs,
            out_specs=pl.BlockSpec((1, width, 1), lambda s, c, pt: (s, 0, 0)),
            scratch_shapes=[pltpu.VMEM((width, page), F32), pltpu.VMEM((nh, 1), F32), pltpu.VMEM((nh, page), F32),
                            pltpu.VMEM((width, page), F32), pltpu.VMEM((nh, page), F32)]),
        out_shape=jax.ShapeDtypeStruct((db, width, 1), F32),
        compiler_params=_params(2),
        name="fox_sample",
    )(page_table, fq.reshape(db, width, 1), fkv_new.reshape(db, 2 * width, 1), lf_new.reshape(db, nh, 1),
      *([kv_t] * pc), *([w_suffix] * pc), *([tot] * pc))
    return out.reshape(db, width)


def _nsa_sample_cmp_kernel(pt_ref, q_ref, wbig_ref, posr_ref, w2k_ref, w2v_ref, gain_ref, msel_ref, utri_ref, *rest,
                           n_pages, past, n_cmp, n_sel):
    del pt_ref
    pages = rest[:n_pages]
    oc_ref, idx_ref = rest[n_pages:n_pages + 2]
    kc_ref, vc_ref = rest[n_pages + 2:n_pages + 4]
    x_refs = rest[n_pages + 4:]
    feat, page = pages[0].shape[1], pages[0].shape[2]
    for j in range(n_pages):
        xt = pages[j][0].T
        for h, x_ref in enumerate(x_refs):
            x_ref[j * page:(j + 1) * page, :] = xt[:, h * LANES:(h + 1) * LANES]
    r = n_pages * page // CMP_STRIDE
    ab = None
    for rr in range(CMP_STRIDE):
        rows = jnp.concatenate([x_ref[pl.ds(rr, r, stride=CMP_STRIDE), :] for x_ref in x_refs], axis=1)
        part = _dot(rows.astype(BF16), wbig_ref[rr * feat:(rr + 1) * feat, :])
        ab = part if ab is None else ab + part
    _compress_finish(ab, wbig_ref, posr_ref, w2k_ref, w2v_ref, gain_ref, kc_ref, vc_ref)
    q8 = q_ref[0]
    row = lax.broadcasted_iota(jnp.int32, (NSA_HEADS, 1), 0)
    lane_row = _lane((1, LANES))
    oc = jnp.zeros((NSA_HEADS, LANES), F32)
    idx_row = jnp.zeros((1, LANES), jnp.int32)
    for k in range(NSA_KV_HEADS):
        o_k, p = _cmp_branch(q8, kc_ref[0, k], vc_ref[0, k], past, n_cmp)
        in_grp = row // NSA_GROUP == k
        oc = jnp.where(in_grp, o_k, oc)
        psum = jnp.broadcast_to(jnp.sum(jnp.where(in_grp, p, 0.0), axis=0, keepdims=True), p.shape)
        hi = psum.astype(BF16)
        lo = (psum - hi.astype(F32)).astype(BF16)
        imp = _dot(hi, msel_ref[...]) + _dot(lo, msel_ref[...])
        sel = _select_blocks(imp, past // SEL_BLOCK, n_sel)
        rank = _dot(sel.astype(BF16), utri_ref[...])
        j = _lane(imp.shape)
        for n in range(TOP_N):
            val = jnp.sum(jnp.where(sel & (rank == n), j, 0), axis=-1, keepdims=True)
            idx_row = jnp.where(lane_row == k * TOP_N + n, val[0:1], idx_row)
    oc_ref[0] = oc
    idx_ref[0] = idx_row


def _nsa_sample_cmp(nqh, cmp_t, page_table, cops, past):
    db = nqh.shape[0]
    n_phys, feat, page = cmp_t.shape
    n_pages = page_table.shape[1]
    r = n_pages * page // CMP_STRIDE
    n_cmp = (past + 1 - CMP_BLOCK) // CMP_STRIDE + 1
    n_sel = past // SEL_BLOCK + 1
    sel_pad = -(-n_sel // LANES) * LANES
    msel = _cmp_to_sel_matrix(r, sel_pad)
    i = np.arange(sel_pad)
    utri = jnp.asarray(i[:, None] < i[None, :], BF16)
    cst = lambda a: pl.BlockSpec(a.shape, lambda s, pt: (0,) * a.ndim)
    in_specs = [pl.BlockSpec((1, NSA_HEADS, LANES), lambda s, pt: (s, 0, 0))] + [cst(a) for a in cops] + [cst(msel), cst(utri)]
    in_specs += [pl.BlockSpec((1, feat, page), lambda s, pt, j=j: (pt[jnp.minimum(s, db - 1), j], 0, 0))
                 for j in range(n_pages)]
    o_c, idx = pl.pallas_call(
        functools.partial(_nsa_sample_cmp_kernel, n_pages=n_pages, past=past, n_cmp=n_cmp, n_sel=n_sel),
        grid_spec=pltpu.PrefetchScalarGridSpec(
            num_scalar_prefetch=1, grid=(db,), in_specs=in_specs,
            out_specs=[pl.BlockSpec((1, NSA_HEADS, LANES), lambda s, pt: (s, 0, 0)),
                       pl.BlockSpec((1, 1, LANES), lambda s, pt: (s, 0, 0))],
            scratch_shapes=[pltpu.VMEM((1, NSA_KV_HEADS, r, LANES), BF16), pltpu.VMEM((1, NSA_KV_HEADS, r, LANES), BF16)]
            + [pltpu.VMEM((n_pages * page, LANES), F32)] * (feat // LANES)),
        out_shape=[jax.ShapeDtypeStruct((db, NSA_HEADS, LANES), F32), jax.ShapeDtypeStruct((db, 1, LANES), jnp.int32)],
        compiler_params=_params(1),
        name="nsa_sample_cmp",
    )(page_table, nqh, *cops, msel, utri, *([cmp_t] * n_pages))
    return o_c, idx[:, 0, :NSA_KV_HEADS * TOP_N]


def _nsa_sample_attend_kernel(pt_ref, idx_ref, qmat_ref, oc_ref, sm_ref, snew_ref, snewc_ref, wnew_ref, wnewc_ref,
                              state_ref, *rest, nb_past, bpp):
    del pt_ref
    n_blk = NSA_KV_HEADS * TOP_N
    pages = rest[:n_blk]
    o_ref, wout_ref = rest[n_blk:]
    s = pl.program_id(0)
    g = NSA_GROUP
    qm = qmat_ref[0]
    feat = qm.shape[1]
    row = lax.broadcasted_iota(jnp.int32, (NSA_HEADS, 1), 0)

    def new_score(col_ref):
        return _dot(qm, jnp.broadcast_to(col_ref[0], (feat, LANES)).astype(BF16))[:, 0:1]

    lane_p = _lane((NSA_HEADS, pages[0].shape[2]))
    tiles = []
    has_new = jnp.zeros((NSA_HEADS, 1), jnp.bool_)
    for k in range(NSA_KV_HEADS):
        in_grp = row // g == k
        for n in range(TOP_N):
            idx = idx_ref[s, k * TOP_N + n]
            is_new = idx >= nb_past
            half = jnp.minimum(idx, nb_past - 1) % bpp
            data = pages[k * TOP_N + n][0].astype(BF16)
            ok = in_grp & jnp.logical_not(is_new) & (lane_p // SEL_BLOCK == half)
            tiles.append((jnp.where(ok, _dot(qm, data), NEG), data))
            has_new = has_new | (in_grp & is_new)
    s_new = jnp.where(has_new, new_score(snewc_ref), NEG)
    m = s_new
    for sc, _ in tiles:
        m = jnp.maximum(m, sc.max(axis=-1, keepdims=True))
    l = jnp.exp(s_new - m)
    acc = l * snew_ref[0]
    for sc, data in tiles:
        p = jnp.exp(sc - m)
        l = l + jnp.sum(p, axis=-1, keepdims=True)
        acc = acc + _dot_t(p.astype(BF16), data)
    o_s = acc / l
    st = state_ref[0]
    w_keep = st.shape[1]
    stb = st.astype(BF16)
    sw = _dot(qm, stb)
    sw = jnp.where(_lane(sw.shape) >= w_keep - WINDOW + 1, sw, NEG)
    s_new = new_score(wnewc_ref)
    m = jnp.maximum(jnp.max(sw, axis=-1, keepdims=True), s_new)
    p = jnp.exp(sw - m)
    p_new = jnp.exp(s_new - m)
    o_w = (_dot_t(p.astype(BF16), stb) + p_new * wnew_ref[0]) / (jnp.sum(p, axis=-1, keepdims=True) + p_new)

    def values_of(a):
        v = a[:, LANES:2 * LANES]
        return jnp.where(row < g, v, pltpu.roll(v, HEAD_DIM, 1))

    smb = jnp.broadcast_to(sm_ref[0], (NSA_HEADS, LANES))
    lane = _lane(smb.shape)
    gate = lambda j: jnp.sum(jnp.where(lane == SM_GATE + 3 * row + j, smb, 0.0), axis=-1, keepdims=True)
    o_ref[0] = gate(0) * oc_ref[0] + gate(1) * values_of(o_s) + gate(2) * values_of(o_w)
    lane_f = _lane((feat, LANES))
    n_chunks = w_keep // LANES
    shifted = [pltpu.roll(st[:, c * LANES:(c + 1) * LANES], LANES - 1, 1) for c in range(n_chunks)]
    shifted.append(jnp.broadcast_to(wnewc_ref[0], (feat, LANES)))
    for c in range(n_chunks):
        wout_ref[0, :, c * LANES:(c + 1) * LANES] = jnp.where(lane_f < LANES - 1, shifted[c], shifted[c + 1])


def _nsa_sample_attend(qmat, o_c, small, skv_new, wkv_new, state_t, sel_t, page_table, sel_idx):
    db = qmat.shape[0]
    n_phys, feat, page = sel_t.shape
    bpp = page // SEL_BLOCK
    nb_past = page_table.shape[1] * bpp
    w_keep = state_t.shape[2]
    assert w_keep % LANES == 0
    n_blk = NSA_KV_HEADS * TOP_N

    def page_map(s, pt, idx, n):
        row = jnp.minimum(s, db - 1)
        return (pt[row, jnp.clip(idx[row, n], 0, nb_past - 1) // bpp], 0, 0)

    per_seq = lambda a, b: pl.BlockSpec((1, a, b), lambda s, pt, idx: (s, 0, 0))
    in_specs = [per_seq(NSA_HEADS, feat), per_seq(NSA_HEADS, LANES), per_seq(1, LANES), per_seq(1, feat), per_seq(feat, 1),
                per_seq(1, feat), per_seq(feat, 1), per_seq(feat, w_keep)]
    in_specs += [pl.BlockSpec((1, feat, page), functools.partial(page_map, n=n)) for n in range(n_blk)]
    return pl.pallas_call(
        functools.partial(_nsa_sample_attend_kernel, nb_past=nb_past, bpp=bpp),
        grid_spec=pltpu.PrefetchScalarGridSpec(
            num_scalar_prefetch=2, grid=(db,), in_specs=in_specs,
            out_specs=[per_seq(NSA_HEADS, LANES), per_seq(feat, w_keep)]),
        out_shape=[jax.ShapeDtypeStruct((db, NSA_HEADS, LANES), F32), jax.ShapeDtypeStruct((db, feat, w_keep), F32)],
        compiler_params=_params(1),
        name="nsa_sample_attend",
    )(page_table, sel_idx, qmat, o_c, small.reshape(db, 1, LANES), skv_new.reshape(db, 1, feat), skv_new.reshape(db, feat, 1),
      wkv_new.reshape(db, 1, feat), wkv_new.reshape(db, feat, 1), state_t, *([sel_t] * n_blk))


def kernel(x_prompt, x_sample, cache_fox_kv, cache_fox_logf, cache_cmp_kv, cache_sel_kv, state_win_kv, page_table, g_attn, w_in, b_forget, b_gate, g_q_fox, g_k_fox, g_q_nsa, g_k_cmp, g_k_sel, g_k_win, pos_cmp_k, w_cmp_k1, w_cmp_k2, pos_cmp_v, w_cmp_v1, w_cmp_v2, g_out_fox, g_out_nsa, w_out, g_ffn, w_router_group, b_router_group, w_router_expert, b_router_expert, w_exp_gate, w_exp_up, w_exp_down):
    assert w_in.shape[0] == 1 and x_sample.shape[1] == 1
    l = 0
    proj_ops = _in_proj_operands(g_attn[l], w_in[l], b_forget[l], b_gate[l], g_q_fox[l], g_k_fox[l], g_q_nsa[l],
                                 g_k_sel[l], g_k_win[l])
    cops = _compress_operands(pos_cmp_k[l], w_cmp_k1[l], w_cmp_k2[l], pos_cmp_v[l], w_cmp_v1[l], w_cmp_v2[l], g_k_cmp[l])
    post_ops, moe_ops = _finish_operands(g_out_fox[l], g_out_nsa[l], w_out[l], g_ffn[l], w_router_group[l],
                                         b_router_group[l], w_router_expert[l], b_router_expert[l], w_exp_gate[l],
                                         w_exp_up[l], w_exp_down[l])
    y_p, fkv_p, lf_p, ckv_p, skv_p, wkv_p = _prompt_group(x_prompt, proj_ops, cops, post_ops, moe_ops)
    y_s, fkv_s, lf_s, ckv_s, skv_s, wkv_s = _sample_group(x_sample, cache_fox_kv[l], cache_fox_logf[l], cache_cmp_kv[l],
                                                          cache_sel_kv[l], state_win_kv[l], page_table, proj_ops, cops,
                                                          post_ops, moe_ops)
    return (y_p, y_s, fkv_p, fkv_s, lf_p, lf_s, ckv_p, ckv_s, skv_p, skv_s, wkv_p, wkv_s)


def _prompt_group(x, proj_ops, cops, post_ops, moe_ops):
    b, s, d = x.shape
    tile = _row_tile(s, 512)
    (fkv, ckv, skv, wkv, small, fq, fk, fv, nq, ksel, vsel, kwin, vwin) = _in_proj_prompt(x, proj_ops, tile)
    o_fox = _fox_prompt(fq, fk, fv, tile)
    kc, vc = _compress_prompt(ckv, cops, b)
    o_nsa = _nsa_prompt(nq, kc, vc, ksel, vsel, kwin, vwin, small, 128)
    y = _finish_layer(o_fox.reshape(b * s, FOX_W), o_nsa.reshape(b * s, NSA_W), x.reshape(b * s, d), post_ops, moe_ops,
                      _row_tile(b * s, 1024))
    win = min(WINDOW, s)
    kvh, hd = NSA_KV_HEADS, HEAD_DIM
    return (y.reshape(b, s, d), fkv.reshape(1, b, s, 2, FOX_HEADS, hd),
            small[:, SM_LOGF:SM_LOGF + FOX_HEADS].reshape(1, b, s, FOX_HEADS),
            ckv.reshape(1, b, s, 2, kvh, hd), skv.reshape(1, b, s, 2, kvh, hd),
            wkv.reshape(b, s, 2, kvh, hd)[:, s - win:].reshape(1, b, win, 2, kvh, hd))


def _sample_group(x, cache_fox_kv, cache_fox_logf, cache_cmp_kv, cache_sel_kv, state_win_kv, page_table, proj_ops, cops,
                  post_ops, moe_ops):
    db, _, d = x.shape
    n_phys, page = cache_fox_kv.shape[0], cache_fox_kv.shape[1]
    n_pages = page_table.shape[1]
    past = n_pages * page
    w_keep = state_win_kv.shape[1]
    kvh, hd = NSA_KV_HEADS, HEAD_DIM
    fox_t = jnp.transpose(cache_fox_kv, (0, 2, 3, 4, 1)).reshape(n_phys, 2, FOX_W, page)
    logf_t = jnp.transpose(cache_fox_logf, (0, 2, 1))
    sel_t = jnp.transpose(cache_sel_kv, (0, 2, 3, 4, 1)).reshape(n_phys, 2 * KV_W, page)
    state_t = jnp.transpose(state_win_kv, (0, 2, 3, 4, 1)).reshape(db, 2 * KV_W, w_keep)
    (fkv, skv, wkv, small, fq, nqh, qmat, fkv_t, ckv_t, skv_t, wkv_t, small_t) = _in_proj_sample(x, proj_ops, past)
    w_suffix, tot = _logf_suffix(logf_t)
    pages_per_step = max(t for t in range(1, 17) if n_pages % t == 0)
    o_fox = _fox_sample(fq, fkv, small[:, SM_LOGF:SM_LOGF + FOX_HEADS], fox_t, w_suffix, tot, page_table, pages_per_step)
    cmp_t = jnp.transpose(cache_cmp_kv, (0, 2, 3, 4, 1)).reshape(n_phys, 2 * KV_W, page)
    o_c, sel_idx = _nsa_sample_cmp(jnp.transpose(nqh, (1, 0, 2)), cmp_t, page_table, cops, past)
    o_nsa, win_t = _nsa_sample_attend(jnp.transpose(qmat, (1, 0, 2)), o_c, small, skv, wkv, state_t, sel_t, page_table,
                                      sel_idx)
    y = _finish_layer(o_fox, o_nsa[:, :, :HEAD_DIM].reshape(db, NSA_W), x.reshape(db, d), post_ops, moe_ops,
                      _row_tile(db, 1024))
    rows = lambda a_t, *dims: jnp.transpose(a_t.reshape(*dims, db), (len(dims),) + tuple(range(len(dims)))).reshape(
        (1, db, 1) + dims)
    return (y.reshape(db, 1, d), rows(fkv_t, 2, FOX_HEADS, hd), rows(small_t[SM_LOGF:SM_LOGF + FOX_HEADS], FOX_HEADS),
            rows(ckv_t, 2, kvh, hd), rows(skv_t, 2, kvh, hd),
            jnp.transpose(win_t.reshape(db, 2, kvh, hd, w_keep), (0, 4, 1, 2, 3))[None])
```

```python
import functools

import jax
import jax.numpy as jnp
import numpy as np
from jax import lax
from jax.experimental import pallas as pl
from jax.experimental.pallas import tpu as pltpu

F32 = jnp.float32
BF16 = jnp.bfloat16

LANES = 128
HEAD_DIM = 64
FOX_HEADS = 8
NSA_HEADS = 8
NSA_KV_HEADS = 2
NSA_GROUP = NSA_HEADS // NSA_KV_HEADS
ROPE_DIM = HEAD_DIM // 4
ROPE_THETA = 500000.0
CMP_BLOCK = 32
CMP_STRIDE = 16
CMP_HIDDEN = 128
SEL_BLOCK = 64
TOP_N = 8
N_LOCAL = 2
WINDOW = 512
N_GROUPS = 4
EXPERTS_PER_GROUP = 4
N_EXPERTS = N_GROUPS * EXPERTS_PER_GROUP
EPS = 1e-6
NEG = -1e30
FORCE = 1e4
SCALE = HEAD_DIM ** -0.5
PAGE_SIZE = 128

FOX_W = FOX_HEADS * HEAD_DIM
NSA_W = NSA_HEADS * HEAD_DIM
KV_W = NSA_KV_HEADS * HEAD_DIM

C_FQ, C_FK, C_FV, C_NQ = 0, 512, 1024, 1536
C_KC, C_VC, C_KS, C_VS, C_KW, C_VW = 2048, 2176, 2304, 2432, 2560, 2688
C_SMALL = 2816
D_IN_PAD = C_SMALL + LANES
SM_LOGF, SM_GATE, SM_CUM = 0, 8, 32


VMEM_LIMIT = 56 * 1024 * 1024


def _params(grid_rank, vmem=VMEM_LIMIT):
    return pltpu.CompilerParams(dimension_semantics=("arbitrary",) * grid_rank, vmem_limit_bytes=vmem)


def _dot(a, b):
    return jnp.dot(a, b, preferred_element_type=F32)


def _dot_t(a, b):
    return lax.dot_general(a, b, (((1,), (1,)), ((), ())), preferred_element_type=F32)


def _split3(x):
    hi = x.astype(BF16)
    r = x - hi.astype(F32)
    mid = r.astype(BF16)
    lo = (r - mid.astype(F32)).astype(BF16)
    return hi, mid, lo


def _dot3(a_f32, b_bf16):
    hi, mid, lo = _split3(a_f32)
    return _dot(hi, b_bf16) + _dot(mid, b_bf16) + _dot(lo, b_bf16)


def _lane(shape):
    return lax.broadcasted_iota(jnp.int32, shape, len(shape) - 1)


def _head_rms(seg, bd, gain):
    ms = _dot((seg * seg).astype(BF16), bd) * (1.0 / HEAD_DIM)
    return seg * lax.rsqrt(ms + EPS) * gain


def _rope(x, cos, sin_m, sin_p):
    return x * cos + pltpu.roll(x, LANES - ROPE_DIM // 2, 1) * sin_m + pltpu.roll(x, ROPE_DIM // 2, 1) * sin_p


def _in_proj_common(x_ref, gat_ref, w_ref, bsm_ref, gq_fox_ref, gk_fox_ref, gq_nsa_ref, gk_sel_ref, gk_win_ref,
                    bd_ref, cos_ref, sinm_ref, sinp_ref):
    x = x_ref[...]
    xn = (x * lax.rsqrt(jnp.mean(x * x, axis=-1, keepdims=True) + EPS) * gat_ref[...]).astype(BF16)
    bd = bd_ref[...]
    bd128 = bd[:LANES, :LANES]
    cos, sin_m, sin_p = cos_ref[...], sinm_ref[...], sinp_ref[...]

    def proj(c0, width):
        return _dot(xn, w_ref[:, c0:c0 + width])

    def normed(c0, gain_ref, chunks):
        outs = []
        for j in range(chunks):
            seg = proj(c0 + 256 * j, 256)
            outs.append(_head_rms(seg, bd, gain_ref[...]))
        return outs

    out = {}
    out["fq"] = normed(C_FQ, gq_fox_ref, 2)
    out["fk"] = normed(C_FK, gk_fox_ref, 2)
    out["fv"] = [proj(C_FV, 256), proj(C_FV + 256, 256)]
    out["nq"] = normed(C_NQ, gq_nsa_ref, 2)
    out["nqr"] = [jnp.concatenate([_rope(c[:, :LANES], cos, sin_m, sin_p), _rope(c[:, LANES:], cos, sin_m, sin_p)], axis=1)
                  for c in out["nq"]]
    out["ckv"] = proj(C_KC, 256)
    ks = proj(C_KS, LANES)
    out["ks"] = _rope(ks * lax.rsqrt(_dot((ks * ks).astype(BF16), bd128) * (1.0 / HEAD_DIM) + EPS) * gk_sel_ref[...],
                      cos, sin_m, sin_p)
    out["vs"] = proj(C_VS, LANES)
    kw = proj(C_KW, LANES)
    out["kw"] = _rope(kw * lax.rsqrt(_dot((kw * kw).astype(BF16), bd128) * (1.0 / HEAD_DIM) + EPS) * gk_win_ref[...],
                      cos, sin_m, sin_p)
    out["vw"] = proj(C_VW, LANES)
    t = proj(C_SMALL, LANES) + bsm_ref[...]
    logf = jnp.minimum(t, 0.0) - jnp.log1p(jnp.exp(-jnp.abs(t)))
    gate = 1.0 / (1.0 + jnp.exp(-t))
    lane = _lane(t.shape)
    out["small"] = jnp.where(lane < SM_GATE, logf, jnp.where(lane < SM_CUM, gate, 0.0))
    return out


def _in_proj_prompt_kernel(x_ref, gat_ref, w_ref, bsm_ref, gq_fox_ref, gk_fox_ref, gq_nsa_ref, gk_sel_ref, gk_win_ref,
                           bd_ref, cos_ref, sinm_ref, sinp_ref, tri_ref,
                           fkv_ref, ckv_ref, skv_ref, wkv_ref, small_ref,
                           fq_ref, fk_ref, fv_ref, nq_ref, ksel_ref, vsel_ref, kwin_ref, vwin_ref, ckv_t_ref, lf_t_ref,
                           carry_ref, *, tiles_per_seq):
    p = _in_proj_common(x_ref, gat_ref, w_ref, bsm_ref, gq_fox_ref, gk_fox_ref, gq_nsa_ref, gk_sel_ref, gk_win_ref,
                        bd_ref, cos_ref, sinm_ref, sinp_ref)
    tm = x_ref.shape[0]
    lane = _lane((tm, LANES))
    lo_half = lane < HEAD_DIM

    @pl.when(pl.program_id(0) % tiles_per_seq == 0)
    def _():
        carry_ref[...] = jnp.zeros_like(carry_ref)

    small = p["small"]
    hi, mid, lo = _split3(small)
    tri = tri_ref[...]
    cum = _dot(tri, hi) + _dot(tri, mid) + _dot(tri, lo) + carry_ref[...]
    carry_ref[...] = cum[tm - 1:tm, :]
    cum_sh = pltpu.roll(cum, SM_CUM - SM_LOGF, 1)
    small_ref[...] = jnp.where((lane >= SM_CUM) & (lane < SM_CUM + FOX_HEADS), cum_sh, small)

    lf_t_ref[0] = small.T[SM_LOGF:SM_LOGF + FOX_HEADS]
    for j in range(2):
        fkv_ref[0, 256 * j:256 * (j + 1), :] = p["fk"][j].T
        fkv_ref[0, 512 + 256 * j:512 + 256 * (j + 1), :] = p["fv"][j].T
    ckv_ref[...] = p["ckv"]
    ckv_t_ref[0] = p["ckv"].T
    skv_ref[0, 0:LANES, :] = p["ks"].T
    skv_ref[0, LANES:2 * LANES, :] = p["vs"].T
    wkv_ref[0, 0:LANES, :] = p["kw"].T
    wkv_ref[0, LANES:2 * LANES, :] = p["vw"].T

    def pair(chunks, j):
        c = chunks[j // 2]
        return c[:, (j % 2) * LANES:(j % 2 + 1) * LANES]

    one = jnp.ones((tm, LANES), F32)
    zero = jnp.zeros((tm, LANES), F32)
    ones_col = jnp.where(lane == HEAD_DIM, one, zero)
    for h in range(FOX_HEADS):
        qc, kc, vc = pair(p["fq"], h // 2), pair(p["fk"], h // 2), pair(p["fv"], h // 2)
        if h % 2:
            qc, kc, vc = pltpu.roll(qc, HEAD_DIM, 1), pltpu.roll(kc, HEAD_DIM, 1), pltpu.roll(vc, HEAD_DIM, 1)
        ch = jnp.broadcast_to(cum[:, SM_LOGF + h:SM_LOGF + h + 1], (tm, LANES))
        c_hi, c_mid, c_lo = _split3(ch)
        c_hi, c_mid, c_lo = c_hi.astype(F32), c_mid.astype(F32), c_lo.astype(F32)
        q_ext = jnp.where(lane == 64, c_hi, jnp.where(lane == 65, c_mid, jnp.where(lane == 66, c_lo,
                          jnp.where(lane < 70, one, zero))))
        k_ext = jnp.where(lane < 67, one, jnp.where(lane == 67, -c_hi, jnp.where(lane == 68, -c_mid,
                          jnp.where(lane == 69, -c_lo, zero))))
        fq_ref[0, h] = jnp.where(lo_half, qc * SCALE, q_ext).astype(BF16)
        fk_ref[0, h] = jnp.where(lo_half, kc, k_ext).astype(BF16)
        fv_ref[0, h] = jnp.where(lo_half, vc, ones_col).astype(BF16)
    for h in range(NSA_HEADS):
        a, r = pair(p["nq"], h // 2), pair(p["nqr"], h // 2)
        if h % 2:
            v = jnp.where(lo_half, pltpu.roll(a, HEAD_DIM, 1), r)
        else:
            v = jnp.where(lo_half, a, pltpu.roll(r, HEAD_DIM, 1))
        nq_ref[0, h] = (v * SCALE).astype(BF16)
    for name_k, name_v, k_ref, v_ref in (("ks", "vs", ksel_ref, vsel_ref), ("kw", "vw", kwin_ref, vwin_ref)):
        kk, vv = p[name_k], p[name_v]
        k_ref[0, 0] = jnp.where(lo_half, zero, pltpu.roll(kk, HEAD_DIM, 1)).astype(BF16)
        k_ref[0, 1] = jnp.where(lo_half, zero, kk).astype(BF16)
        v_ref[0, 0] = jnp.where(lo_half, vv, ones_col).astype(BF16)
        v_ref[0, 1] = jnp.where(lo_half, pltpu.roll(vv, HEAD_DIM, 1), ones_col).astype(BF16)


def _in_proj_sample_kernel(x_ref, gat_ref, w_ref, bsm_ref, gq_fox_ref, gk_fox_ref, gq_nsa_ref, gk_sel_ref, gk_win_ref,
                           bd_ref, cos_ref, sinm_ref, sinp_ref,
                           fkv_ref, skv_ref, wkv_ref, small_ref, fq_ref, nqh_ref, qmat_ref,
                           fkv_t_ref, ckv_t_ref, skv_t_ref, wkv_t_ref, small_t_ref):
    p = _in_proj_common(x_ref, gat_ref, w_ref, bsm_ref, gq_fox_ref, gk_fox_ref, gq_nsa_ref, gk_sel_ref, gk_win_ref,
                        bd_ref, cos_ref, sinm_ref, sinp_ref)
    small_ref[...] = p["small"]
    small_t_ref[...] = p["small"].T
    for j in range(2):
        fkv_ref[:, 256 * j:256 * (j + 1)] = p["fk"][j]
        fkv_ref[:, 512 + 256 * j:512 + 256 * (j + 1)] = p["fv"][j]
        fkv_t_ref[256 * j:256 * (j + 1), :] = p["fk"][j].T
        fkv_t_ref[512 + 256 * j:512 + 256 * (j + 1), :] = p["fv"][j].T
        fq_ref[:, 256 * j:256 * (j + 1)] = p["fq"][j] * SCALE
    ckv_t_ref[...] = p["ckv"].T
    for ref, ref_t, k_name, v_name in ((skv_ref, skv_t_ref, "ks", "vs"), (wkv_ref, wkv_t_ref, "kw", "vw")):
        ref[:, 0:LANES] = p[k_name]
        ref[:, LANES:2 * LANES] = p[v_name]
        ref_t[0:LANES, :] = p[k_name].T
        ref_t[LANES:2 * LANES, :] = p[v_name].T
    tm = x_ref.shape[0]
    lo_half = _lane((tm, LANES)) < HEAD_DIM
    zero = jnp.zeros((tm, LANES), F32)
    for h in range(NSA_HEADS):
        a = p["nq"][h // 4][:, (h // 2 % 2) * LANES:(h // 2 % 2 + 1) * LANES]
        r = p["nqr"][h // 4][:, (h // 2 % 2) * LANES:(h // 2 % 2 + 1) * LANES]
        if h % 2:
            v = jnp.where(lo_half, pltpu.roll(a, HEAD_DIM, 1), r)
        else:
            v = jnp.where(lo_half, a, pltpu.roll(r, HEAD_DIM, 1))
        nqh_ref[h] = (v * SCALE).astype(BF16)
        kvh = h // NSA_GROUP
        rr = r if (h % 2) == kvh else pltpu.roll(r, HEAD_DIM, 1)
        rr = jnp.where(lo_half, rr, zero) if kvh == 0 else jnp.where(lo_half, zero, rr)
        qmat_ref[h, :, 0:LANES] = (rr * SCALE).astype(BF16)
        qmat_ref[h, :, LANES:2 * LANES] = zero.astype(BF16)


def _const_spec(shape):
    nd = len(shape)
    return pl.BlockSpec(shape, lambda *_: (0,) * nd)


def _rope_tables(pos):
    half = ROPE_DIM // 2
    inv = ROPE_THETA ** (-jnp.arange(half, dtype=F32) * 2.0 / ROPE_DIM)
    ang = pos.astype(F32)[:, None] * inv[None, :]
    cos, sin = jnp.cos(ang), jnp.sin(ang)
    n = pos.shape[0]
    rest = HEAD_DIM - ROPE_DIM
    cos_h = jnp.concatenate([cos, cos, jnp.ones((n, rest), F32)], axis=1)
    sinm_h = jnp.concatenate([-sin, jnp.zeros((n, half + rest), F32)], axis=1)
    sinp_h = jnp.concatenate([jnp.zeros((n, half), F32), sin, jnp.zeros((n, rest), F32)], axis=1)
    return tuple(jnp.concatenate([t, t], axis=1) for t in (cos_h, sinm_h, sinp_h))


def _in_proj_operands(g_attn, w_in, b_forget, b_gate, g_q_fox, g_k_fox, g_q_nsa, g_k_sel, g_k_win):
    d_model = w_in.shape[0]
    sizes = (FOX_W, FOX_W, FOX_W, FOX_HEADS, NSA_W, KV_W, KV_W, KV_W, KV_W, KV_W, KV_W, 3 * NSA_HEADS)
    offs = np.concatenate([[0], np.cumsum(sizes)])
    fq, fk, fv, ff, nq, kc, vc, ks, vs, kw, vw, ng = [w_in[:, offs[i]:offs[i + 1]] for i in range(len(sizes))]
    pad = jnp.zeros((d_model, LANES - FOX_HEADS - 3 * NSA_HEADS), w_in.dtype)
    w = jnp.concatenate([fq, fk, fv, nq, kc, vc, ks, vs, kw, vw, ff, ng, pad], axis=1).astype(BF16)
    bsm = jnp.concatenate([b_forget, b_gate, jnp.zeros((LANES - FOX_HEADS - 3 * NSA_HEADS,), F32)]).reshape(1, LANES)
    hd = np.arange(256) // HEAD_DIM
    bd = jnp.asarray(hd[:, None] == hd[None, :], BF16)
    t256 = lambda g: jnp.tile(g, 256 // HEAD_DIM).reshape(1, 256)
    t128 = lambda g: jnp.tile(g, LANES // HEAD_DIM).reshape(1, LANES)
    return (g_attn.reshape(1, d_model), w, bsm, t256(g_q_fox), t256(g_k_fox), t256(g_q_nsa), t128(g_k_sel), t128(g_k_win), bd)


def _in_proj_prompt(x, ops, tm):
    b, s, d = x.shape
    n = b * s
    tps = s // tm
    cos, sinm, sinp = _rope_tables(jnp.arange(s, dtype=jnp.int32))
    tri = jnp.asarray(np.tril(np.ones((tm, tm), np.float32)), BF16)
    tab_spec = pl.BlockSpec((tm, LANES), lambda i: (i % tps, 0))
    in_specs = ([pl.BlockSpec((tm, d), lambda i: (i, 0))] + [_const_spec(o.shape) for o in ops]
                + [tab_spec, tab_spec, tab_spec, _const_spec((tm, tm))])
    row = lambda w: pl.BlockSpec((tm, w), lambda i: (i, 0))
    head = lambda nh: pl.BlockSpec((1, nh, tm, LANES), lambda i: (i // tps, 0, i % tps, 0))
    hshape = lambda nh: jax.ShapeDtypeStruct((b, nh, s, LANES), BF16)
    feat = lambda w: pl.BlockSpec((1, w, tm), lambda i: (i // tps, 0, i % tps))
    fshape = lambda w: jax.ShapeDtypeStruct((b, w, s), F32)
    out_specs = [feat(2 * FOX_W), row(2 * KV_W), feat(2 * KV_W), feat(2 * KV_W), row(LANES),
                 head(FOX_HEADS), head(FOX_HEADS), head(FOX_HEADS), head(NSA_HEADS),
                 head(NSA_KV_HEADS), head(NSA_KV_HEADS), head(NSA_KV_HEADS), head(NSA_KV_HEADS),
                 feat(2 * KV_W), feat(FOX_HEADS)]
    out_shape = [fshape(2 * FOX_W), jax.ShapeDtypeStruct((n, 2 * KV_W), F32), fshape(2 * KV_W), fshape(2 * KV_W),
                 jax.ShapeDtypeStruct((n, LANES), F32), hshape(FOX_HEADS), hshape(FOX_HEADS), hshape(FOX_HEADS),
                 hshape(NSA_HEADS), hshape(NSA_KV_HEADS), hshape(NSA_KV_HEADS), hshape(NSA_KV_HEADS), hshape(NSA_KV_HEADS),
                 fshape(2 * KV_W), fshape(FOX_HEADS)]
    return pl.pallas_call(
        functools.partial(_in_proj_prompt_kernel, tiles_per_seq=tps),
        grid=(n // tm,), in_specs=in_specs, out_specs=out_specs, out_shape=out_shape,
        scratch_shapes=[pltpu.VMEM((1, LANES), F32)],
        compiler_params=_params(1),
        name="in_proj_prompt",
    )(x.reshape(n, d), *ops, cos, sinm, sinp, tri)


def _in_proj_sample(x, ops, past):
    b, q, d = x.shape
    n = b * q
    pos = past + jnp.arange(q, dtype=jnp.int32)
    cos, sinm, sinp = (jnp.tile(t, (b, 1)) for t in _rope_tables(pos))
    full = lambda shape: pl.BlockSpec(shape, lambda i: (0, 0))
    in_specs = [full((n, d))] + [_const_spec(o.shape) for o in ops] + [full((n, LANES))] * 3
    widths = (2 * FOX_W, 2 * KV_W, 2 * KV_W, LANES, FOX_W)
    widths_t = (2 * FOX_W, 2 * KV_W, 2 * KV_W, 2 * KV_W, LANES)
    hm = lambda w: pl.BlockSpec((NSA_HEADS, n, w), lambda i: (0, 0, 0))
    return pl.pallas_call(
        _in_proj_sample_kernel, grid=(1,), in_specs=in_specs,
        out_specs=[full((n, w)) for w in widths] + [hm(LANES), hm(2 * LANES)] + [full((w, n)) for w in widths_t],
        out_shape=[jax.ShapeDtypeStruct((n, w), F32) for w in widths] + [
            jax.ShapeDtypeStruct((NSA_HEADS, n, LANES), BF16), jax.ShapeDtypeStruct((NSA_HEADS, n, 2 * LANES), BF16)] + [
            jax.ShapeDtypeStruct((w, n), F32) for w in widths_t],
        compiler_params=_params(1),
        name="in_proj_sample",
    )(x.reshape(n, d), *ops, cos, sinm, sinp)


def _flash_step(q, k, v, m, acc, mask=None):
    s = _dot_t(q, k)
    if mask is not None:
        s = jnp.where(mask, s, NEG)
    m_new = jnp.maximum(m, jnp.max(s, axis=-1, keepdims=True))
    p = jnp.exp(s - m_new)
    acc = acc * jnp.exp(m - m_new) + _dot(p.astype(BF16), v)
    return m_new, acc


def _finish(acc):
    return acc / acc[:, HEAD_DIM:HEAD_DIM + 1]


def _pack_heads(a, b):
    return jnp.where(_lane(a.shape) < HEAD_DIM, a, pltpu.roll(b, HEAD_DIM, 1))


def _fox_prompt_kernel(q_ref, k_ref, v_ref, o_ref, *, tile):
    qi = pl.program_id(2)
    rowi = lax.broadcasted_iota(jnp.int32, (tile, tile), 0)
    coli = lax.broadcasted_iota(jnp.int32, (tile, tile), 1)
    outs = []
    for j in range(2):
        q = q_ref[0, j]

        def body(t, carry, j=j, q=q):
            off = pl.multiple_of(t * tile, tile)
            return _flash_step(q, k_ref[0, j, pl.ds(off, tile), :], v_ref[0, j, pl.ds(off, tile), :], *carry)

        init = (jnp.full((tile, 1), NEG, F32), jnp.zeros((tile, LANES), F32))
        m, acc = lax.fori_loop(0, qi, body, init)
        off = pl.multiple_of(qi * tile, tile)
        m, acc = _flash_step(q, k_ref[0, j, pl.ds(off, tile), :], v_ref[0, j, pl.ds(off, tile), :], m, acc,
                             mask=coli <= rowi)
        outs.append(_finish(acc))
    o_ref[0] = _pack_heads(outs[0], outs[1])


def _fox_prompt(fq, fk, fv, tile):
    b, h, s, _ = fq.shape
    qspec = pl.BlockSpec((1, 2, tile, LANES), lambda bi, hi, qi: (bi, hi, qi, 0))
    kspec = pl.BlockSpec((1, 2, s, LANES), lambda bi, hi, qi: (bi, hi, 0, 0))
    return pl.pallas_call(
        functools.partial(_fox_prompt_kernel, tile=tile),
        grid=(b, h // 2, s // tile), in_specs=[qspec, kspec, kspec],
        out_specs=pl.BlockSpec((1, tile, LANES), lambda bi, hi, qi: (bi, qi, hi)),
        out_shape=jax.ShapeDtypeStruct((b, s, h * HEAD_DIM), F32),
        compiler_params=_params(3),
        name="fox_prompt",
    )(fq, fk, fv)


def _compress_operands(pos_k, w_k1, w_k2, pos_v, w_v1, w_v2, g_k_cmp):
    half = CMP_BLOCK // 2
    wk = jnp.stack([w_k1, w_k1, w_v1, w_v1]).reshape(4, 2, half, HEAD_DIM, CMP_HIDDEN)
    wbig = jnp.einsum("kardh,kj->rkdajh", wk, jnp.eye(4, dtype=w_k1.dtype))
    wbig = wbig.reshape(half * 4 * HEAD_DIM, 2 * 4 * CMP_HIDDEN).astype(BF16)
    pos = jnp.stack([pos_k, pos_k, pos_v, pos_v]).reshape(4, 2, half, HEAD_DIM)
    pos_rows = jnp.transpose(pos, (1, 2, 0, 3)).reshape(2, half * 4 * HEAD_DIM)
    pos_rows = jnp.concatenate([pos_rows, jnp.zeros((6, pos_rows.shape[1]), pos_rows.dtype)]).astype(BF16)
    pad2 = lambda w: jnp.pad(w, ((0, 0), (0, LANES - HEAD_DIM))).astype(BF16)
    gain = jnp.pad(g_k_cmp, (0, LANES - HEAD_DIM)).reshape(1, LANES)
    return wbig, pos_rows, pad2(w_k2), pad2(w_v2), gain


def _compress_core(x, w_ref, pos_ref, w2k_ref, w2v_ref, gain_ref, kc_ref, vc_ref):
    _compress_finish(_dot(x.astype(BF16), w_ref[...]), w_ref, pos_ref, w2k_ref, w2v_ref, gain_ref, kc_ref, vc_ref)


def _compress_finish(ab, w_ref, pos_ref, w2k_ref, w2v_ref, gain_ref, kc_ref, vc_ref):
    r = ab.shape[0]
    hw = 4 * CMP_HIDDEN
    pw = _dot(pos_ref[...], w_ref[...])
    hid = ab[:, :hw] + pltpu.roll(ab[:, hw:], r - 1, 0) + pw[0:1, :hw] + pw[1:2, hw:]
    hid = (hid * (1.0 / (1.0 + jnp.exp(-hid)))).astype(BF16)
    lane = _lane((r, LANES))
    for kvh in range(NSA_KV_HEADS):
        kk = _dot(hid[:, kvh * CMP_HIDDEN:(kvh + 1) * CMP_HIDDEN], w2k_ref[...])
        ms = jnp.sum(kk * kk, axis=-1, keepdims=True) * (1.0 / HEAD_DIM)
        kc_ref[0, kvh] = (kk * lax.rsqrt(ms + EPS) * gain_ref[...]).astype(BF16)
        vv = _dot(hid[:, (2 + kvh) * CMP_HIDDEN:(3 + kvh) * CMP_HIDDEN], w2v_ref[...])
        vc_ref[0, kvh] = jnp.where(lane == HEAD_DIM, 1.0, vv).astype(BF16)


def _compress_prompt_kernel(x_ref, w_ref, pos_ref, w2k_ref, w2v_ref, gain_ref, kc_ref, vc_ref):
    _compress_core(x_ref[0], w_ref, pos_ref, w2k_ref, w2v_ref, gain_ref, kc_ref, vc_ref)


def _compress_prompt(ckv, cops, b):
    n = ckv.shape[0]
    r = n // b // CMP_STRIDE
    width = CMP_STRIDE * ckv.shape[1]
    out = jax.ShapeDtypeStruct((b, NSA_KV_HEADS, r, LANES), BF16)
    ospec = pl.BlockSpec((1, NSA_KV_HEADS, r, LANES), lambda i: (i, 0, 0, 0))
    return pl.pallas_call(
        _compress_prompt_kernel, grid=(b,),
        in_specs=[pl.BlockSpec((1, r, width), lambda i: (i, 0, 0))] + [_const_spec(o.shape) for o in cops],
        out_specs=[ospec, ospec], out_shape=[out, out],
        compiler_params=_params(1),
        name="compress_prompt",
    )(ckv.reshape(b, r, width), *cops)


def _cmp_branch(q, kc, vc, qpos_rows, n_valid_limit):
    s = _dot_t(q, kc)
    cidx = _lane(s.shape)
    valid = (cidx * CMP_STRIDE + (CMP_BLOCK - 1) <= qpos_rows) & (cidx < n_valid_limit)
    s = jnp.where(valid, s, NEG)
    p = jnp.exp(s - jnp.max(s, axis=-1, keepdims=True)) * valid.astype(F32)
    l = jnp.sum(p, axis=-1, keepdims=True)
    p = p / jnp.where(l > 0, l, 1.0)
    return _dot(p.astype(BF16), vc), p


def _select_blocks(imp, qblk, n_sel):
    j = _lane(imp.shape)
    dist = qblk - j
    forced = (j == 0) | ((dist >= 0) & (dist < N_LOCAL))
    score = jnp.where(dist >= 0, jnp.where(forced, FORCE, imp), NEG)
    score = jnp.where(j < n_sel, score, -jnp.inf)
    cnt = jnp.zeros(imp.shape, F32)
    for i in range(n_sel):
        col = score[:, i:i + 1]
        beats = (col > score) | ((col == score) & (j > i))
        cnt = cnt + beats.astype(F32)
    return (cnt < min(TOP_N, n_sel)) & (j < n_sel)


def _select_blocks_t(imp_t, qblk, n_sel):
    j = lax.broadcasted_iota(jnp.int32, imp_t.shape, 0)
    dist = qblk - j
    forced = (j == 0) | ((dist >= 0) & (dist < N_LOCAL))
    score = jnp.where(dist >= 0, jnp.where(forced, FORCE, imp_t), NEG)
    cnt = jnp.zeros(imp_t.shape, F32)
    for i in range(n_sel):
        row = jnp.broadcast_to(score[i:i + 1, :], score.shape)
        beats = (row > score) | ((row == score) & (j > i))
        cnt = cnt + beats.astype(F32)
    return cnt < min(TOP_N, n_sel)


def _nsa_prompt_kernel(q_ref, kc_ref, vc_ref, ks_ref, vs_ref, kw_ref, vw_ref, sm_ref, msel_ref, exp_ref, o_ref, *,
                       tq, tk, n_cmp, n_sel):
    i = pl.program_id(1)
    start = i * tq
    g = NSA_GROUP
    rows = g * tq
    s_len = ks_ref.shape[2]
    qpos = start + lax.broadcasted_iota(jnp.int32, (tq, 1), 0)
    qpos_rows = jnp.concatenate([qpos] * g, axis=0)
    qblk_t = (start + lax.broadcasted_iota(jnp.int32, (1, tq), 1)) // SEL_BLOCK
    small = sm_ref[...]
    kpos_l = lax.broadcasted_iota(jnp.int32, (tq, tk), 1)
    band = min(WINDOW + tq, s_len)
    kpos_w = lax.broadcasted_iota(jnp.int32, (tq, band), 1)
    init = (jnp.full((rows, 1), NEG, F32), jnp.zeros((rows, LANES), F32))
    heads = []
    for k in range(NSA_KV_HEADS):
        q = q_ref[0, k * g:(k + 1) * g].reshape(rows, LANES)
        o_c, p = _cmp_branch(q, kc_ref[0, k], vc_ref[0, k], qpos_rows, n_cmp)
        psum = p[0:tq]
        for gi in range(1, g):
            psum = psum + p[gi * tq:(gi + 1) * tq]
        hi = psum.astype(BF16)
        lo = (psum - hi.astype(F32)).astype(BF16)
        imp = _dot(hi, msel_ref[...]) + _dot(lo, msel_ref[...])
        sel_t = _select_blocks_t(imp.T[0:n_sel], qblk_t, n_sel)
        sel_t = jnp.concatenate([sel_t.astype(F32), jnp.zeros((LANES - n_sel, tq), F32)], axis=0)
        selb = sel_t.T.astype(BF16)

        def sel_body(t, carry, k=k, q=q, selb=selb):
            off = pl.multiple_of(t * tk, tk)
            picked = _dot(selb, exp_ref[t]) > 0.5
            ok = picked & (t * tk + kpos_l <= qpos)
            s = _dot_t(q, ks_ref[0, k, pl.ds(off, tk), :]).reshape(g, tq, tk)
            s = jnp.where(ok[None], s, NEG).reshape(rows, tk)
            m, acc = carry
            m_new = jnp.maximum(m, jnp.max(s, axis=-1, keepdims=True))
            pr = jnp.exp(s - m_new)
            acc = acc * jnp.exp(m - m_new) + _dot(pr.astype(BF16), vs_ref[0, k, pl.ds(off, tk), :])
            return m_new, acc

        _, acc_s = lax.fori_loop(0, (start + tq + tk - 1) // tk, sel_body, init)
        off = pl.multiple_of(jnp.maximum(start + tq - band, 0), tq)
        diff = qpos - (off + kpos_w)
        ok = (diff >= 0) & (diff < WINDOW)
        s = _dot_t(q, kw_ref[0, k, pl.ds(off, band), :]).reshape(g, tq, band)
        s = jnp.where(ok[None], s, NEG).reshape(rows, band)
        pr = jnp.exp(s - jnp.max(s, axis=-1, keepdims=True))
        acc_w = _dot(pr.astype(BF16), vw_ref[0, k, pl.ds(off, band), :])
        o_s, o_w = _finish(acc_s), _finish(acc_w)
        for gi in range(g):
            h = k * g + gi
            sl = slice(gi * tq, (gi + 1) * tq)
            c0 = SM_GATE + 3 * h
            heads.append(small[:, c0:c0 + 1] * o_c[sl] + small[:, c0 + 1:c0 + 2] * o_s[sl] + small[:, c0 + 2:c0 + 3] * o_w[sl])
    for c in range(NSA_HEADS // 2):
        o_ref[0, :, c * LANES:(c + 1) * LANES] = _pack_heads(heads[2 * c], heads[2 * c + 1])


def _cmp_to_sel_matrix(n_rows, n_sel):
    lead = CMP_BLOCK // CMP_STRIDE - 1
    r = SEL_BLOCK // CMP_STRIDE
    c = np.arange(n_rows)[:, None]
    j = np.arange(n_sel)[None, :]
    return jnp.asarray((c >= r * j - lead) & (c <= r * j + r - 1), BF16)


def _nsa_prompt(nq, kc, vc, ksel, vsel, kwin, vwin, small, tq):
    b, h, s, _ = nq.shape
    nb = s // tq
    r = kc.shape[2]
    n_cmp = (s - CMP_BLOCK) // CMP_STRIDE + 1
    n_sel = s // SEL_BLOCK
    assert n_sel <= LANES
    tk = _row_tile(s, 512)
    msel = _cmp_to_sel_matrix(r, LANES)
    key_blk = (np.arange(s) // SEL_BLOCK).reshape(s // tk, 1, tk)
    expand = jnp.asarray(np.arange(LANES).reshape(1, LANES, 1) == key_blk, BF16)
    seq_spec = lambda rows: pl.BlockSpec((1, NSA_KV_HEADS, rows, LANES), lambda bi, qi: (bi, 0, 0, 0))
    return pl.pallas_call(
        functools.partial(_nsa_prompt_kernel, tq=tq, tk=tk, n_cmp=n_cmp, n_sel=n_sel),
        grid=(b, nb),
        in_specs=[pl.BlockSpec((1, h, tq, LANES), lambda bi, qi: (bi, 0, qi, 0)), seq_spec(r), seq_spec(r),
                  seq_spec(s), seq_spec(s), seq_spec(s), seq_spec(s),
                  pl.BlockSpec((tq, LANES), lambda bi, qi: (bi * nb + qi, 0)), _const_spec(msel.shape),
                  _const_spec(expand.shape)],
        out_specs=pl.BlockSpec((1, tq, h * HEAD_DIM), lambda bi, qi: (bi, qi, 0)),
        out_shape=jax.ShapeDtypeStruct((b, s, h * HEAD_DIM), F32),
        compiler_params=_params(2),
        name="nsa_prompt",
    )(nq, kc, vc, ksel, vsel, kwin, vwin, small, msel, expand)


def _rms(x, gain):
    return x * lax.rsqrt(jnp.mean(x * x, axis=-1, keepdims=True) + EPS) * gain


def _first_lane_where(cond, lane):
    return jnp.min(jnp.where(cond, lane, LANES), axis=-1, keepdims=True)


def _post_attn_kernel(of_ref, on_ref, x_ref, gf_ref, gn_ref, wo_ref, gffn_ref, wr_hi_ref, wr_lo_ref, br_ref,
                      x1_ref, h_ref, gate_ref):
    nf = _rms(of_ref[...], gf_ref[...]).astype(BF16)
    nn = _rms(on_ref[...], gn_ref[...]).astype(BF16)
    x1 = x_ref[...] + _dot(nf, wo_ref[0:FOX_W, :]) + _dot(nn, wo_ref[FOX_W:FOX_W + NSA_W, :])
    x1_ref[...] = x1
    h = _rms(x1, gffn_ref[...])
    h_ref[...] = h.astype(BF16)
    h_hi = h.astype(BF16)
    h_lo = (h - h_hi.astype(F32)).astype(BF16)
    logits = _dot(h_hi, wr_hi_ref[...]) + _dot(h_lo, wr_hi_ref[...]) + _dot(h_hi, wr_lo_ref[...]) + br_ref[...]
    lane = _lane(logits.shape)
    is_grp = (lane >= N_EXPERTS) & (lane < N_EXPERTS + N_GROUPS)
    lg = jnp.where(is_grp, logits, NEG)
    lg_max = jnp.max(lg, axis=-1, keepdims=True)
    gsel = _first_lane_where(is_grp & (lg == lg_max), lane) - N_EXPERTS
    pg_sel = 1.0 / jnp.sum(jnp.where(is_grp, jnp.exp(lg - lg_max), 0.0), axis=-1, keepdims=True)
    in_grp = (lane >= gsel * EXPERTS_PER_GROUP) & (lane < (gsel + 1) * EXPERTS_PER_GROUP)
    le = jnp.where(in_grp, logits, NEG)
    pe = jnp.where(in_grp, jnp.exp(le - jnp.max(le, axis=-1, keepdims=True)), 0.0)
    pe = pe / jnp.sum(pe, axis=-1, keepdims=True)
    pe = jnp.where(in_grp, pe, -1.0)
    v1 = jnp.max(pe, axis=-1, keepdims=True)
    i1 = _first_lane_where(pe == v1, lane)
    rest = jnp.where(lane == i1, -1.0, pe)
    v2 = jnp.max(rest, axis=-1, keepdims=True)
    i2 = _first_lane_where(rest == v2, lane)
    tot = v1 + v2
    gate_ref[...] = jnp.where(lane == i1, v1 / tot * pg_sel, jnp.where(lane == i2, v2 / tot * pg_sel, 0.0))


def _moe_kernel(h_ref, gate_ref, x1_ref, wgu_ref, wd_ref, y_ref):
    e = pl.program_id(1)

    @pl.when(e == 0)
    def _():
        y_ref[...] = x1_ref[...]

    d_exp = wd_ref.shape[1]
    gu = _dot(h_ref[...], wgu_ref[0])
    gpre, up = gu[:, :d_exp], gu[:, d_exp:]
    gate = gate_ref[...]
    ge = jnp.sum(jnp.where(_lane(gate.shape) == e, gate, 0.0), axis=-1, keepdims=True)
    hid = gpre * (1.0 / (1.0 + jnp.exp(-gpre))) * up * ge
    y_ref[...] += _dot(hid.astype(BF16), wd_ref[0])


def _finish_operands(g_out_fox, g_out_nsa, w_out, g_ffn, w_rg, b_rg, w_re, b_re, w_gate, w_up, w_down):
    d = w_out.shape[1]
    pad = LANES - N_EXPERTS - N_GROUPS
    wr = jnp.concatenate([w_re, w_rg, jnp.zeros((d, pad), F32)], axis=1)
    wr_hi = wr.astype(BF16)
    wr_lo = (wr - wr_hi.astype(F32)).astype(BF16)
    br = jnp.concatenate([b_re, b_rg, jnp.zeros((pad,), F32)]).reshape(1, LANES)
    post = (g_out_fox.reshape(1, -1), g_out_nsa.reshape(1, -1), w_out.astype(BF16), g_ffn.reshape(1, d), wr_hi, wr_lo, br)
    moe = (jnp.concatenate([w_gate, w_up], axis=-1).astype(BF16), w_down.astype(BF16))
    return post, moe


def _finish_layer(o_fox, o_nsa, x, post_ops, moe_ops, tm):
    n, d = x.shape
    tp = _row_tile(n, 512)
    row = lambda w: pl.BlockSpec((tp, w), lambda i: (i, 0))
    x1, h, gate = pl.pallas_call(
        _post_attn_kernel, grid=(n // tp,),
        in_specs=[row(FOX_W), row(NSA_W), row(d)] + [_const_spec(o.shape) for o in post_ops],
        out_specs=[row(d), row(d), row(LANES)],
        out_shape=[jax.ShapeDtypeStruct((n, d), F32), jax.ShapeDtypeStruct((n, d), BF16), jax.ShapeDtypeStruct((n, LANES), F32)],
        compiler_params=_params(1),
        name="post_attn",
    )(o_fox, o_nsa, x, *post_ops)
    wgu, wd = moe_ops
    tok = lambda w: pl.BlockSpec((tm, w), lambda i, e: (i, 0))
    return pl.pallas_call(
        _moe_kernel, grid=(n // tm, wgu.shape[0]),
        in_specs=[tok(d), tok(LANES), tok(d),
                  pl.BlockSpec((1,) + wgu.shape[1:], lambda i, e: (e, 0, 0)),
                  pl.BlockSpec((1,) + wd.shape[1:], lambda i, e: (e, 0, 0))],
        out_specs=tok(d), out_shape=jax.ShapeDtypeStruct((n, d), F32),
        compiler_params=_params(2),
        name="moe",
    )(h, gate, x1, wgu, wd)


def _row_tile(n, cap):
    for t in range(min(cap, n), 7, -1):
        if n % t == 0 and t % 8 == 0:
            return t
    return n


def _logf_suffix_kernel(lf_ref, u_ref, ones_ref, w_ref, tot_ref):
    hi, mid, lo = _split3(lf_ref[...])
    u, o = u_ref[...], ones_ref[...]
    w_ref[...] = _dot(hi, u) + _dot(mid, u) + _dot(lo, u)
    tot_ref[...] = _dot(hi, o) + _dot(mid, o) + _dot(lo, o)


def _logf_suffix(logf_t):
    n_phys, nh, page = logf_t.shape
    rows = n_phys * nh
    t = np.arange(page)
    u = jnp.asarray(t[:, None] > t[None, :], BF16)
    ones = jnp.ones((page, page), BF16)
    tr = _row_tile(rows, 2048)
    spec = pl.BlockSpec((tr, page), lambda i: (i, 0))
    out = jax.ShapeDtypeStruct((rows, page), F32)
    w, tot = pl.pallas_call(
        _logf_suffix_kernel, grid=(rows // tr,),
        in_specs=[spec, _const_spec(u.shape), _const_spec(ones.shape)],
        out_specs=[spec, spec], out_shape=[out, out],
        compiler_params=_params(1),
        name="logf_suffix",
    )(logf_t.reshape(rows, page), u, ones)
    return w.reshape(n_phys, nh, page), tot.reshape(n_phys, nh, page)


def _per_head(col):
    nh, w = col.shape
    return jnp.broadcast_to(col[:, None, :], (nh, HEAD_DIM, w)).reshape(nh * HEAD_DIM, w)


def _head_sums(x):
    return jnp.sum(x.reshape(x.shape[0] // HEAD_DIM, HEAD_DIM, x.shape[1]), axis=1)


def _fox_sample_kernel(pt_ref, q_ref, new_ref, lfn_ref, *rest, pc):
    del pt_ref
    kv_refs, w_refs, tot_refs = rest[:pc], rest[pc:2 * pc], rest[2 * pc:3 * pc]
    o_ref = rest[3 * pc]
    qb_ref, m_ref, carry_ref, acc_ref, l_ref = rest[3 * pc + 1:]
    c = pl.program_id(1)
    page = acc_ref.shape[1]

    @pl.when(c == 0)
    def _():
        qb_ref[...] = jnp.broadcast_to(q_ref[0], qb_ref.shape)
        m_ref[...] = jnp.full(m_ref.shape, NEG, F32)
        carry_ref[...] = jnp.zeros_like(carry_ref)
        acc_ref[...] = jnp.zeros_like(acc_ref)
        l_ref[...] = jnp.zeros_like(l_ref)

    qb = qb_ref[...]
    carry = carry_ref[...]
    scores = []
    for j in range(pc):
        scores.append(_head_sums(kv_refs[j][0, 0] * qb) + (lfn_ref[0] + carry + w_refs[j][0]))
        carry = carry + tot_refs[j][0]
    carry_ref[...] = carry
    m_chunk = scores[0]
    for s in scores[1:]:
        m_chunk = jnp.maximum(m_chunk, s)
    m_old = m_ref[...]
    m_new = jnp.maximum(m_old, jnp.max(m_chunk, axis=-1, keepdims=True))
    m_ref[...] = m_new
    alpha = jnp.exp(m_old - m_new)
    acc = acc_ref[...] * _per_head(alpha)
    l = l_ref[...] * alpha
    for j in range(pc):
        p = jnp.exp(scores[j] - m_new)
        l = l + p
        acc = acc + _per_head(p) * kv_refs[j][0, 1]
    acc_ref[...] = acc
    l_ref[...] = l

    @pl.when(c == pl.num_programs(1) - 1)
    def _():
        width = qb.shape[0]
        k_new, v_new = new_ref[0, 0:width, :], new_ref[0, width:2 * width, :]
        s_new = _head_sums(k_new * q_ref[0])
        m_fin = jnp.maximum(m_new, s_new)
        a = jnp.exp(m_new - m_fin)
        p_new = jnp.exp(s_new - m_fin)
        l_tot = jnp.sum(l, axis=-1, keepdims=True) * a + p_new
        acc_col = jnp.sum(acc, axis=-1, keepdims=True) * _per_head(a) + _per_head(p_new) * v_new
        o_ref[0] = acc_col / _per_head(l_tot)


def _fox_sample(fq, fkv_new, lf_new, kv_t, w_suffix, tot, page_table, pc):
    db = fq.shape[0]
    n_phys, _, width, page = kv_t.shape
    nh = w_suffix.shape[1]
    n_pages = page_table.shape[1]
    phys = lambda s, c, pt, j: pt[jnp.minimum(s, db - 1), jnp.clip(n_pages - 1 - (c * pc + j), 0, n_pages - 1)]
    per_seq = lambda r: pl.BlockSpec((1, r, 1), lambda s, c, pt: (s, 0, 0))
    in_specs = [per_seq(width), per_seq(2 * width), per_seq(nh)]
    in_specs += [pl.BlockSpec((1, 2, width, page), lambda s, c, pt, j=j: (phys(s, c, pt, j), 0, 0, 0)) for j in range(pc)]
    in_specs += [pl.BlockSpec((1, nh, page), lambda s, c, pt, j=j: (phys(s, c, pt, j), 0, 0)) for j in list(range(pc)) * 2]
    out = pl.pallas_call(
        functools.partial(_fox_sample_kernel, pc=pc),
        grid_spec=pltpu.PrefetchScalarGridSpec(
            num_scalar_prefetch=1, grid=(db, n_pages // pc), in_specs=in_specs,
            out_specs=pl.BlockSpec((1, width, 1), lambda s, c, pt: (s, 0, 0)),
            scratch_shapes=[pltpu.VMEM((width, page), F32), pltpu.VMEM((nh, 1), F32), pltpu.VMEM((nh, page), F32),
                            pltpu.VMEM((width, page), F32), pltpu.VMEM((nh, page), F32)]),
        out_shape=jax.ShapeDtypeStruct((db, width, 1), F32),
        compiler_params=_params(2),
        name="fox_sample",
    )(page_table, fq.reshape(db, width, 1), fkv_new.reshape(db, 2 * width, 1), lf_new.reshape(db, nh, 1),
      *([kv_t] * pc), *([w_suffix] * pc), *([tot] * pc))
    return out.reshape(db, width)


def _nsa_sample_cmp_kernel(pt_ref, q_ref, wbig_ref, posr_ref, w2k_ref, w2v_ref, gain_ref, msel_ref, utri_ref, *rest,
                           n_pages, past, n_cmp, n_sel):
    del pt_ref
    pages = rest[:n_pages]
    oc_ref, idx_ref = rest[n_pages:n_pages + 2]
    kc_ref, vc_ref = rest[n_pages + 2:n_pages + 4]
    x_refs = rest[n_pages + 4:]
    feat, page = pages[0].shape[1], pages[0].shape[2]
    for j in range(n_pages):
        xt = pages[j][0].T
        for h, x_ref in enumerate(x_refs):
            x_ref[j * page:(j + 1) * page, :] = xt[:, h * LANES:(h + 1) * LANES]
    r = n_pages * page // CMP_STRIDE
    ab = None
    for rr in range(CMP_STRIDE):
        rows = jnp.concatenate([x_ref[pl.ds(rr, r, stride=CMP_STRIDE), :] for x_ref in x_refs], axis=1)
        part = _dot(rows.astype(BF16), wbig_ref[rr * feat:(rr + 1) * feat, :])
        ab = part if ab is None else ab + part
    _compress_finish(ab, wbig_ref, posr_ref, w2k_ref, w2v_ref, gain_ref, kc_ref, vc_ref)
    q8 = q_ref[0]
    row = lax.broadcasted_iota(jnp.int32, (NSA_HEADS, 1), 0)
    lane_row = _lane((1, LANES))
    oc = jnp.zeros((NSA_HEADS, LANES), F32)
    idx_row = jnp.zeros((1, LANES), jnp.int32)
    for k in range(NSA_KV_HEADS):
        o_k, p = _cmp_branch(q8, kc_ref[0, k], vc_ref[0, k], past, n_cmp)
        in_grp = row // NSA_GROUP == k
        oc = jnp.where(in_grp, o_k, oc)
        psum = jnp.broadcast_to(jnp.sum(jnp.where(in_grp, p, 0.0), axis=0, keepdims=True), p.shape)
        hi = psum.astype(BF16)
        lo = (psum - hi.astype(F32)).astype(BF16)
        imp = _dot(hi, msel_ref[...]) + _dot(lo, msel_ref[...])
        sel = _select_blocks(imp, past // SEL_BLOCK, n_sel)
        rank = _dot(sel.astype(BF16), utri_ref[...])
        j = _lane(imp.shape)
        for n in range(TOP_N):
            val = jnp.sum(jnp.where(sel & (rank == n), j, 0), axis=-1, keepdims=True)
            idx_row = jnp.where(lane_row == k * TOP_N + n, val[0:1], idx_row)
    oc_ref[0] = oc
    idx_ref[0] = idx_row


def _nsa_sample_cmp(nqh, cmp_t, page_table, cops, past):
    db = nqh.shape[0]
    n_phys, feat, page = cmp_t.shape
    n_pages = page_table.shape[1]
    r = n_pages * page // CMP_STRIDE
    n_cmp = (past + 1 - CMP_BLOCK) // CMP_STRIDE + 1
    n_sel = past // SEL_BLOCK + 1
    sel_pad = -(-n_sel // LANES) * LANES
    msel = _cmp_to_sel_matrix(r, sel_pad)
    i = np.arange(sel_pad)
    utri = jnp.asarray(i[:, None] < i[None, :], BF16)
    cst = lambda a: pl.BlockSpec(a.shape, lambda s, pt: (0,) * a.ndim)
    in_specs = [pl.BlockSpec((1, NSA_HEADS, LANES), lambda s, pt: (s, 0, 0))] + [cst(a) for a in cops] + [cst(msel), cst(utri)]
    in_specs += [pl.BlockSpec((1, feat, page), lambda s, pt, j=j: (pt[jnp.minimum(s, db - 1), j], 0, 0))
                 for j in range(n_pages)]
    o_c, idx = pl.pallas_call(
        functools.partial(_nsa_sample_cmp_kernel, n_pages=n_pages, past=past, n_cmp=n_cmp, n_sel=n_sel),
        grid_spec=pltpu.PrefetchScalarGridSpec(
            num_scalar_prefetch=1, grid=(db,), in_specs=in_specs,
            out_specs=[pl.BlockSpec((1, NSA_HEADS, LANES), lambda s, pt: (s, 0, 0)),
                       pl.BlockSpec((1, 1, LANES), lambda s, pt: (s, 0, 0))],
            scratch_shapes=[pltpu.VMEM((1, NSA_KV_HEADS, r, LANES), BF16), pltpu.VMEM((1, NSA_KV_HEADS, r, LANES), BF16)]
            + [pltpu.VMEM((n_pages * page, LANES), F32)] * (feat // LANES)),
        out_shape=[jax.ShapeDtypeStruct((db, NSA_HEADS, LANES), F32), jax.ShapeDtypeStruct((db, 1, LANES), jnp.int32)],
        compiler_params=_params(1),
        name="nsa_sample_cmp",
    )(page_table, nqh, *cops, msel, utri, *([cmp_t] * n_pages))
    return o_c, idx[:, 0, :NSA_KV_HEADS * TOP_N]


def _nsa_sample_attend_kernel(pt_ref, idx_ref, qmat_ref, oc_ref, sm_ref, snew_ref, snewc_ref, wnew_ref, wnewc_ref,
                              state_ref, *rest, nb_past, bpp):
    del pt_ref
    n_blk = NSA_KV_HEADS * TOP_N
    pages = rest[:n_blk]
    o_ref, wout_ref = rest[n_blk:]
    s = pl.program_id(0)
    g = NSA_GROUP
    qm = qmat_ref[0]
    feat = qm.shape[1]
    row = lax.broadcasted_iota(jnp.int32, (NSA_HEADS, 1), 0)

    def new_score(col_ref):
        return _dot(qm, jnp.broadcast_to(col_ref[0], (feat, LANES)).astype(BF16))[:, 0:1]

    lane_p = _lane((NSA_HEADS, pages[0].shape[2]))
    tiles = []
    has_new = jnp.zeros((NSA_HEADS, 1), jnp.bool_)
    for k in range(NSA_KV_HEADS):
        in_grp = row // g == k
        for n in range(TOP_N):
            idx = idx_ref[s, k * TOP_N + n]
            is_new = idx >= nb_past
            half = jnp.minimum(idx, nb_past - 1) % bpp
            data = pages[k * TOP_N + n][0].astype(BF16)
            ok = in_grp & jnp.logical_not(is_new) & (lane_p // SEL_BLOCK == half)
            tiles.append((jnp.where(ok, _dot(qm, data), NEG), data))
            has_new = has_new | (in_grp & is_new)
    s_new = jnp.where(has_new, new_score(snewc_ref), NEG)
    m = s_new
    for sc, _ in tiles:
        m = jnp.maximum(m, sc.max(axis=-1, keepdims=True))
    l = jnp.exp(s_new - m)
    acc = l * snew_ref[0]
    for sc, data in tiles:
        p = jnp.exp(sc - m)
        l = l + jnp.sum(p, axis=-1, keepdims=True)
        acc = acc + _dot_t(p.astype(BF16), data)
    o_s = acc / l
    st = state_ref[0]
    w_keep = st.shape[1]
    stb = st.astype(BF16)
    sw = _dot(qm, stb)
    sw = jnp.where(_lane(sw.shape) >= w_keep - WINDOW + 1, sw, NEG)
    s_new = new_score(wnewc_ref)
    m = jnp.maximum(jnp.max(sw, axis=-1, keepdims=True), s_new)
    p = jnp.exp(sw - m)
    p_new = jnp.exp(s_new - m)
    o_w = (_dot_t(p.astype(BF16), stb) + p_new * wnew_ref[0]) / (jnp.sum(p, axis=-1, keepdims=True) + p_new)

    def values_of(a):
        v = a[:, LANES:2 * LANES]
        return jnp.where(row < g, v, pltpu.roll(v, HEAD_DIM, 1))

    smb = jnp.broadcast_to(sm_ref[0], (NSA_HEADS, LANES))
    lane = _lane(smb.shape)
    gate = lambda j: jnp.sum(jnp.where(lane == SM_GATE + 3 * row + j, smb, 0.0), axis=-1, keepdims=True)
    o_ref[0] = gate(0) * oc_ref[0] + gate(1) * values_of(o_s) + gate(2) * values_of(o_w)
    lane_f = _lane((feat, LANES))
    n_chunks = w_keep // LANES
    shifted = [pltpu.roll(st[:, c * LANES:(c + 1) * LANES], LANES - 1, 1) for c in range(n_chunks)]
    shifted.append(jnp.broadcast_to(wnewc_ref[0], (feat, LANES)))
    for c in range(n_chunks):
        wout_ref[0, :, c * LANES:(c + 1) * LANES] = jnp.where(lane_f < LANES - 1, shifted[c], shifted[c + 1])


def _nsa_sample_attend(qmat, o_c, small, skv_new, wkv_new, state_t, sel_t, page_table, sel_idx):
    db = qmat.shape[0]
    n_phys, feat, page = sel_t.shape
    bpp = page // SEL_BLOCK
    nb_past = page_table.shape[1] * bpp
    w_keep = state_t.shape[2]
    assert w_keep % LANES == 0
    n_blk = NSA_KV_HEADS * TOP_N

    def page_map(s, pt, idx, n):
        row = jnp.minimum(s, db - 1)
        return (pt[row, jnp.clip(idx[row, n], 0, nb_past - 1) // bpp], 0, 0)

    per_seq = lambda a, b: pl.BlockSpec((1, a, b), lambda s, pt, idx: (s, 0, 0))
    in_specs = [per_seq(NSA_HEADS, feat), per_seq(NSA_HEADS, LANES), per_seq(1, LANES), per_seq(1, feat), per_seq(feat, 1),
                per_seq(1, feat), per_seq(feat, 1), per_seq(feat, w_keep)]
    in_specs += [pl.BlockSpec((1, feat, page), functools.partial(page_map, n=n)) for n in range(n_blk)]
    return pl.pallas_call(
        functools.partial(_nsa_sample_attend_kernel, nb_past=nb_past, bpp=bpp),
        grid_spec=pltpu.PrefetchScalarGridSpec(
            num_scalar_prefetch=2, grid=(db,), in_specs=in_specs,
            out_specs=[per_seq(NSA_HEADS, LANES), per_seq(feat, w_keep)]),
        out_shape=[jax.ShapeDtypeStruct((db, NSA_HEADS, LANES), F32), jax.ShapeDtypeStruct((db, feat, w_keep), F32)],
        compiler_params=_params(1),
        name="nsa_sample_attend",
    )(page_table, sel_idx, qmat, o_c, small.reshape(db, 1, LANES), skv_new.reshape(db, 1, feat), skv_new.reshape(db, feat, 1),
      wkv_new.reshape(db, 1, feat), wkv_new.reshape(db, feat, 1), state_t, *([sel_t] * n_blk))


def kernel(x_prompt, x_sample, cache_fox_kv, cache_fox_logf, cache_cmp_kv, cache_sel_kv, state_win_kv, page_table, g_attn, w_in, b_forget, b_gate, g_q_fox, g_k_fox, g_q_nsa, g_k_cmp, g_k_sel, g_k_win, pos_cmp_k, w_cmp_k1, w_cmp_k2, pos_cmp_v, w_cmp_v1, w_cmp_v2, g_out_fox, g_out_nsa, w_out, g_ffn, w_router_group, b_router_group, w_router_expert, b_router_expert, w_exp_gate, w_exp_up, w_exp_down):
    assert w_in.shape[0] == 1 and x_sample.shape[1] == 1
    l = 0
    proj_ops = _in_proj_operands(g_attn[l], w_in[l], b_forget[l], b_gate[l], g_q_fox[l], g_k_fox[l], g_q_nsa[l],
                                 g_k_sel[l], g_k_win[l])
    cops = _compress_operands(pos_cmp_k[l], w_cmp_k1[l], w_cmp_k2[l], pos_cmp_v[l], w_cmp_v1[l], w_cmp_v2[l], g_k_cmp[l])
    post_ops, moe_ops = _finish_operands(g_out_fox[l], g_out_nsa[l], w_out[l], g_ffn[l], w_router_group[l],
                                         b_router_group[l], w_router_expert[l], b_router_expert[l], w_exp_gate[l],
                                         w_exp_up[l], w_exp_down[l])
    y_p, fkv_p, lf_p, ckv_p, skv_p, wkv_p = _prompt_group(x_prompt, proj_ops, cops, post_ops, moe_ops)
    y_s, fkv_s, lf_s, ckv_s, skv_s, wkv_s = _sample_group(x_sample, cache_fox_kv[l], cache_fox_logf[l], cache_cmp_kv[l],
                                                          cache_sel_kv[l], state_win_kv[l], page_table, proj_ops, cops,
                                                          post_ops, moe_ops)
    return (y_p, y_s, fkv_p, fkv_s, lf_p, lf_s, ckv_p, ckv_s, skv_p, skv_s, wkv_p, wkv_s)


def _prompt_group(x, proj_ops, cops, post_ops, moe_ops):
    b, s, d = x.shape
    tile = _row_tile(s, 512)
    (fkv_t, ckv, skv_t, wkv_t, small, fq, fk, fv, nq, ksel, vsel, kwin, vwin, ckv_t, lf_t) = _in_proj_prompt(x, proj_ops, tile)
    o_fox = _fox_prompt(fq, fk, fv, tile)
    kc, vc = _compress_prompt(ckv, cops, b)
    o_nsa = _nsa_prompt(nq, kc, vc, ksel, vsel, kwin, vwin, small, 128)
    y = _finish_layer(o_fox.reshape(b * s, FOX_W), o_nsa.reshape(b * s, NSA_W), x.reshape(b * s, d), post_ops, moe_ops,
                      _row_tile(b * s, 1024))
    win = min(WINDOW, s)
    kvh, hd = NSA_KV_HEADS, HEAD_DIM
    rows = lambda a_t, *dims: jnp.transpose(a_t.reshape((b,) + dims + (a_t.shape[-1],)),
                                            (0, len(dims) + 1) + tuple(range(1, len(dims) + 1)))[None]
    return (y.reshape(b, s, d), rows(fkv_t, 2, FOX_HEADS, hd), rows(lf_t, FOX_HEADS), rows(ckv_t, 2, kvh, hd),
            rows(skv_t, 2, kvh, hd), rows(wkv_t[:, :, s - win:], 2, kvh, hd))


def _sample_group(x, cache_fox_kv, cache_fox_logf, cache_cmp_kv, cache_sel_kv, state_win_kv, page_table, proj_ops, cops,
                  post_ops, moe_ops):
    db, _, d = x.shape
    n_phys, page = cache_fox_kv.shape[0], cache_fox_kv.shape[1]
    n_pages = page_table.shape[1]
    past = n_pages * page
    w_keep = state_win_kv.shape[1]
    kvh, hd = NSA_KV_HEADS, HEAD_DIM
    fox_t = jnp.transpose(cache_fox_kv, (0, 2, 3, 4, 1)).reshape(n_phys, 2, FOX_W, page)
    logf_t = jnp.transpose(cache_fox_logf, (0, 2, 1))
    sel_t = jnp.transpose(cache_sel_kv, (0, 2, 3, 4, 1)).reshape(n_phys, 2 * KV_W, page)
    state_t = jnp.transpose(state_win_kv, (0, 2, 3, 4, 1)).reshape(db, 2 * KV_W, w_keep)
    (fkv, skv, wkv, small, fq, nqh, qmat, fkv_t, ckv_t, skv_t, wkv_t, small_t) = _in_proj_sample(x, proj_ops, past)
    w_suffix, tot = _logf_suffix(logf_t)
    pages_per_step = max(t for t in range(1, 17) if n_pages % t == 0)
    o_fox = _fox_sample(fq, fkv, small[:, SM_LOGF:SM_LOGF + FOX_HEADS], fox_t, w_suffix, tot, page_table, pages_per_step)
    cmp_t = jnp.transpose(cache_cmp_kv, (0, 2, 3, 4, 1)).reshape(n_phys, 2 * KV_W, page)
    o_c, sel_idx = _nsa_sample_cmp(jnp.transpose(nqh, (1, 0, 2)), cmp_t, page_table, cops, past)
    o_nsa, win_t = _nsa_sample_attend(jnp.transpose(qmat, (1, 0, 2)), o_c, small, skv, wkv, state_t, sel_t, page_table,
                                      sel_idx)
    y = _finish_layer(o_fox, o_nsa[:, :, :HEAD_DIM].reshape(db, NSA_W), x.reshape(db, d), post_ops, moe_ops,
                      _row_tile(db, 1024))
    rows = lambda a_t, *dims: jnp.transpose(a_t.reshape(*dims, db), (len(dims),) + tuple(range(len(dims)))).reshape(
        (1, db, 1) + dims)
    return (y.reshape(db, 1, d), rows(fkv_t, 2, FOX_HEADS, hd), rows(small_t[SM_LOGF:SM_LOGF + FOX_HEADS], FOX_HEADS),
            rows(ckv_t, 2, kvh, hd), rows(skv_t, 2, kvh, hd),
            jnp.transpose(win_t.reshape(db, 2, kvh, hd, w_keep), (0, 4, 1, 2, 3))[None])
```
